```python
import jax, jax.numpy as jnp
from jax import lax
import numpy as np

D_MODEL = 1024
BATCH = 8
SEQ = 4096
DEPTH = 1
DEC_BATCH = 128
DEC_SEQ = 4
PAST_LEN = 8192
PAGE_SIZE = 128

HEAD_DIM = 64
MIX_WIDTH = D_MODEL
RET_WIDTH = MIX_WIDTH // 2
SB_WIDTH = MIX_WIDTH - RET_WIDTH
N_RET_HEADS = RET_WIDTH // HEAD_DIM
N_SB_HEADS = SB_WIDTH // HEAD_DIM
IN_WIDTH = 4 * RET_WIDTH + 4 * SB_WIDTH
N_META = 16
CHUNK = 128
Q_BLOCK = 128
ROPE_BASE = 10000.0
SB_BIAS_INIT = -6.0
EPS = 1e-6

kernel_name = "hymba_retention_stickbreak_step"


def rms_norm(x, w):
    x32 = x.astype(jnp.float32)
    y = x32 * lax.rsqrt(jnp.mean(x32 * x32, axis=-1, keepdims=True) + EPS)
    return (y * w.astype(jnp.float32)).astype(x.dtype)


def rope(x, pos):
    half = HEAD_DIM // 2
    inv = ROPE_BASE ** (-jnp.arange(half, dtype=jnp.float32) / half)
    ang = pos.astype(jnp.float32)[:, None] * inv[None, :]
    cos = jnp.cos(ang)[None, :, None, :]
    sin = jnp.sin(ang)[None, :, None, :]
    x1, x2 = x[..., :half], x[..., half:]
    return jnp.concatenate([x1 * cos - x2 * sin, x1 * sin + x2 * cos], axis=-1).astype(x.dtype)


def retention_log_decay():
    return jnp.log1p(-jnp.exp2(-5.0 - jnp.arange(N_RET_HEADS, dtype=jnp.float32)))


def chunk_retention(q, k, v, state, log_gamma):
    c = q.shape[1]
    idx = jnp.arange(c, dtype=jnp.float32)
    rel = idx[:, None] - idx[None, :]
    decay = jnp.where(rel[None] >= 0, jnp.exp(jnp.maximum(rel, 0.0)[None] * log_gamma[:, None, None]), 0.0)
    scores = jnp.einsum('bchd,bshd->bhcs', q, k) * decay[None]
    inner = jnp.einsum('bhcs,bshe->bche', scores, v)
    cross_decay = jnp.exp((idx + 1.0)[:, None] * log_gamma[None, :])[None, :, :, None]
    cross = jnp.einsum('bchd,bhde->bche', q, state) * cross_decay
    k_dec = k * jnp.exp((c - 1.0 - idx)[:, None] * log_gamma[None, :])[None, :, :, None]
    new_state = jnp.exp(c * log_gamma)[None, :, None, None] * state + jnp.einsum('bshd,bshe->bhde', k_dec, v)
    return inner + cross, new_state


def sb_attend(q, q_pos, k, v, k_pos, bias):
    z = jnp.einsum('bqhd,bshd->bhqs', q, k).astype(jnp.float32) * (HEAD_DIM ** -0.5) \
        + bias.astype(jnp.float32)[None, :, None, None]
    visible = (k_pos[None, :] < q_pos[:, None])[None, None]
    log_keep = jnp.where(visible, jax.nn.log_sigmoid(-z), 0.0)
    after = lax.cumsum(log_keep, axis=3, reverse=True) - log_keep
    a = jnp.where(visible, jnp.exp(jax.nn.log_sigmoid(z) + after), 0.0)
    return jnp.einsum('bhqs,bshd->bqhd', a.astype(v.dtype), v)


def project(x, norm_w, w_in, q_norm_w, k_norm_w, pos):
    b, t, _ = x.shape
    z = rms_norm(x, norm_w) @ w_in
    cuts = np.cumsum([RET_WIDTH] * 4 + [SB_WIDTH] * 3)
    qr, kr, vr, gr, qs, ks, vs, gs = jnp.split(z, cuts, axis=-1)
    qr = rope(qr.reshape(b, t, N_RET_HEADS, HEAD_DIM), pos)
    kr = rope(kr.reshape(b, t, N_RET_HEADS, HEAD_DIM), pos) * (HEAD_DIM ** -0.5)
    vr = vr.reshape(b, t, N_RET_HEADS, HEAD_DIM)
    qs = rms_norm(qs.reshape(b, t, N_SB_HEADS, HEAD_DIM), q_norm_w)
    ks = rms_norm(ks.reshape(b, t, N_SB_HEADS, HEAD_DIM), k_norm_w)
    vs = vs.reshape(b, t, N_SB_HEADS, HEAD_DIM)
    return qr, kr, vr, gr, qs, ks, vs, gs


def merge(x, o_r, g_r, o_s, g_s, ret_onorm_w, sb_onorm_w, w_out):
    b, t, _ = x.shape
    y_r = rms_norm(o_r, ret_onorm_w.reshape(N_RET_HEADS, HEAD_DIM)).reshape(b, t, RET_WIDTH) * jax.nn.silu(g_r)
    y_s = rms_norm(o_s, sb_onorm_w.reshape(N_SB_HEADS, HEAD_DIM)).reshape(b, t, SB_WIDTH) * jax.nn.silu(g_s)
    y = jnp.concatenate([y_r, y_s], axis=-1).astype(x.dtype)
    return x + y @ w_out


def setup_inputs(seed: int = 0) -> dict:
    key = jax.random.key(seed)
    ks = jax.random.split(key, 16)
    n_pages = PAST_LEN // PAGE_SIZE
    n_used = DEC_BATCH * n_pages
    n_pool = n_used + (n_used + 3) // 4
    page_table = jax.random.permutation(ks[0], n_pool)[:n_used].reshape(DEC_BATCH, n_pages).astype(jnp.int32)
    f32 = jnp.float32
    return {
        "x_prompt": jax.random.normal(ks[1], (BATCH, SEQ, D_MODEL), f32),
        "x_sample": jax.random.normal(ks[2], (DEC_BATCH, DEC_SEQ, D_MODEL), f32),
        "cache_k": jax.random.normal(ks[3], (DEPTH, n_pool, PAGE_SIZE, N_SB_HEADS, HEAD_DIM), f32),
        "cache_v": jax.random.normal(ks[4], (DEPTH, n_pool, PAGE_SIZE, N_SB_HEADS, HEAD_DIM), f32),
        "state_ret": 0.5 * jax.random.normal(ks[5], (DEPTH, DEC_BATCH, N_RET_HEADS, HEAD_DIM, HEAD_DIM), f32),
        "page_table": page_table,
        "meta_tokens": jax.random.normal(ks[6], (N_META, D_MODEL), f32),
        "norm_w": 1.0 + 0.02 * jax.random.normal(ks[7], (DEPTH, D_MODEL), f32),
        "w_in": jax.random.normal(ks[8], (DEPTH, D_MODEL, IN_WIDTH), f32) * D_MODEL ** -0.5,
        "q_norm_w": 1.0 + 0.02 * jax.random.normal(ks[9], (DEPTH, HEAD_DIM), f32),
        "k_norm_w": 1.0 + 0.02 * jax.random.normal(ks[10], (DEPTH, HEAD_DIM), f32),
        "sb_bias": SB_BIAS_INIT + 0.5 * jax.random.normal(ks[14], (DEPTH, N_SB_HEADS), f32),
        "ret_onorm_w": 1.0 + 0.02 * jax.random.normal(ks[11], (DEPTH, RET_WIDTH), f32),
        "sb_onorm_w": 1.0 + 0.02 * jax.random.normal(ks[12], (DEPTH, SB_WIDTH), f32),
        "w_out": jax.random.normal(ks[13], (DEPTH, MIX_WIDTH, D_MODEL), f32) * MIX_WIDTH ** -0.5,
    }


def reference(x_prompt, x_sample, cache_k, cache_v, state_ret, page_table, meta_tokens, norm_w, w_in,
              q_norm_w, k_norm_w, sb_bias, ret_onorm_w, sb_onorm_w, w_out):
    log_gamma = retention_log_decay()

    b, seq, _ = x_prompt.shape
    h = jnp.concatenate([jnp.broadcast_to(meta_tokens.astype(x_prompt.dtype)[None], (b, N_META, D_MODEL)), x_prompt], axis=1)
    n = seq + N_META
    pos = jnp.arange(n, dtype=jnp.int32)
    n_blk = seq // CHUNK

    def to_blocks(a):
        return jnp.moveaxis(a[:, N_META:].reshape(b, n_blk, CHUNK, a.shape[2], HEAD_DIM), 1, 0)

    def from_blocks(a):
        return jnp.moveaxis(a, 0, 1).reshape(b, seq, a.shape[3], HEAD_DIM)

    kp_list, vp_list, sp_list = [], [], []
    for l in range(DEPTH):
        qr, kr, vr, gr, qs, ks, vs, gs = project(h, norm_w[l], w_in[l], q_norm_w[l], k_norm_w[l], pos)
        s0 = jnp.zeros((b, N_RET_HEADS, HEAD_DIM, HEAD_DIM), jnp.float32)
        o_meta, s_meta = chunk_retention(qr[:, :N_META], kr[:, :N_META], vr[:, :N_META], s0, log_gamma)

        def ret_step(state, c):
            o, state = chunk_retention(c[0], c[1], c[2], state, log_gamma)
            return state, o

        s_fin, o_chunks = lax.scan(ret_step, s_meta, (to_blocks(qr), to_blocks(kr), to_blocks(vr)))
        o_r = jnp.concatenate([o_meta, from_blocks(o_chunks)], axis=1)
        bias_l = sb_bias[l]
        o_sb_meta = sb_attend(qs[:, :N_META], pos[:N_META], ks[:, :N_META], vs[:, :N_META], pos[:N_META], bias_l)

        def sb_block(args):
            q_blk, i = args
            q_pos = N_META + i * Q_BLOCK + jnp.arange(Q_BLOCK, dtype=jnp.int32)
            return sb_attend(q_blk, q_pos, ks, vs, pos, bias_l)

        o_sb = lax.map(sb_block, (to_blocks(qs), jnp.arange(n_blk, dtype=jnp.int32)))
        o_s = jnp.concatenate([o_sb_meta, from_blocks(o_sb)], axis=1)
        h = merge(h, o_r, gr, o_s, gs, ret_onorm_w[l], sb_onorm_w[l], w_out[l])
        kp_list.append(ks)
        vp_list.append(vs)
        sp_list.append(s_fin)
    y_prompt = h[:, N_META:]

    db, t, _ = x_sample.shape
    n_pages = page_table.shape[1]
    past_len = n_pages * cache_k.shape[2]
    pos_s = past_len + jnp.arange(t, dtype=jnp.int32)
    k_pos = jnp.arange(past_len + t, dtype=jnp.int32)
    h = x_sample
    ks_list, vs_list, ss_list = [], [], []
    for l in range(DEPTH):
        qr, kr, vr, gr, qs, ks, vs, gs = project(h, norm_w[l], w_in[l], q_norm_w[l], k_norm_w[l], pos_s)
        o_r, s_new = chunk_retention(qr, kr, vr, state_ret[l], log_gamma)
        past_k = cache_k[l][page_table].reshape(db, past_len, N_SB_HEADS, HEAD_DIM)
        past_v = cache_v[l][page_table].reshape(db, past_len, N_SB_HEADS, HEAD_DIM)
        k_all = jnp.concatenate([past_k, ks.astype(past_k.dtype)], axis=1)
        v_all = jnp.concatenate([past_v, vs.astype(past_v.dtype)], axis=1)
        o_s = sb_attend(qs, pos_s, k_all, v_all, k_pos, sb_bias[l])
        h = merge(h, o_r, gr, o_s, gs, ret_onorm_w[l], sb_onorm_w[l], w_out[l])
        ks_list.append(ks)
        vs_list.append(vs)
        ss_list.append(s_new)
    y_sample = h

    new_k_prompt = jnp.stack(kp_list)
    new_v_prompt = jnp.stack(vp_list)
    new_state_ret_prompt = jnp.stack(sp_list)
    new_k_sample = jnp.stack(ks_list)
    new_v_sample = jnp.stack(vs_list)
    new_state_ret_sample = jnp.stack(ss_list)
    return (y_prompt, y_sample, new_k_prompt, new_v_prompt, new_state_ret_prompt, new_k_sample, new_v_sample, new_state_ret_sample)
```

```python
import functools

import numpy as np
import jax
import jax.numpy as jnp
from jax import lax
from jax.experimental import pallas as pl
from jax.experimental.pallas import tpu as pltpu

F32 = jnp.float32
BF16 = jnp.bfloat16

D_MODEL = 1024
HEAD_DIM = 64
N_HEADS = 8
WIDTH = N_HEADS * HEAD_DIM
N_SEG = 8
N_PAIR = N_HEADS // 2
LANES = 128
N_META = 16
CHUNK = 128
ROPE_BASE = 10000.0
EPS = 1e-6
QK_SCALE = HEAD_DIM ** -0.5
VMEM_LIMIT = 56 * 1024 * 1024


def _nt_dot(a, b):
    return lax.dot_general(a, b, (((1,), (1,)), ((), ())), preferred_element_type=F32)


def _tn_dot(a, b):
    return lax.dot_general(a, b, (((0,), (0,)), ((), ())), preferred_element_type=F32)


def _split_bf16(x):
    hi = x.astype(BF16)
    lo = (x - hi.astype(F32)).astype(BF16)
    return jnp.concatenate([hi, lo], axis=1)


def _pair_head_mean(sq, g2):
    return jnp.dot(_split_bf16(sq), g2, preferred_element_type=F32)


def _log_sigmoids(z):
    t = jnp.log(1.0 + jnp.exp(-jnp.abs(z)))
    return jnp.minimum(z, 0.0) - t, -jnp.maximum(z, 0.0) - t


def _silu(g):
    return g * (1.0 / (1.0 + jnp.exp(-g)))


def _rope_pair(z, cos, sin_signed):
    lane = lax.broadcasted_iota(jnp.int32, z.shape, 1)
    first_half = (lane % HEAD_DIM) < (HEAD_DIM // 2)
    swapped = jnp.where(first_half,
                        pltpu.roll(z, LANES - HEAD_DIM // 2, axis=1),
                        pltpu.roll(z, HEAD_DIM // 2, axis=1))
    return z * cos + swapped * sin_signed


def _proj_kernel(x_ref, nw_ref, w_ref, cos_ref, sin_ref, qnw_ref, knw_ref, g2_ref,
                 qr_ref, kr_ref, vr_ref, gr_ref, qs_ref, ks_ref, vs_ref, gs_ref,
                 ksb_ref, vsb_ref):
    x = x_ref[0]
    ms = jnp.mean(x * x, axis=-1, keepdims=True)
    xn = ((x * lax.rsqrt(ms + EPS)) * nw_ref[...]).astype(BF16)
    cos = cos_ref[...]
    sin = sin_ref[...]
    g2 = g2_ref[...]

    def seg(i):
        return jnp.dot(xn, w_ref[:, i * WIDTH:(i + 1) * WIDTH], preferred_element_type=F32)

    def pair(z, p):
        return z[:, p * LANES:(p + 1) * LANES]

    z = seg(0)
    for p in range(N_PAIR):
        qr_ref[0, :, p * LANES:(p + 1) * LANES] = _rope_pair(pair(z, p), cos, sin).astype(BF16)
    z = seg(1)
    for p in range(N_PAIR):
        kr_ref[0, :, p * LANES:(p + 1) * LANES] = (
            _rope_pair(pair(z, p), cos, sin) * QK_SCALE).astype(BF16)
    vr_ref[0] = seg(2).astype(BF16)
    gr_ref[0] = seg(3)
    z = seg(4)
    for p in range(N_PAIR):
        zp = pair(z, p)
        r = lax.rsqrt(_pair_head_mean(zp * zp, g2) + EPS)
        qs_ref[0, :, p * LANES:(p + 1) * LANES] = (
            (zp * r) * qnw_ref[...] * QK_SCALE).astype(BF16)
    z = seg(5)
    for p in range(N_PAIR):
        zp = pair(z, p)
        r = lax.rsqrt(_pair_head_mean(zp * zp, g2) + EPS)
        kn = (zp * r) * knw_ref[...]
        ks_ref[0, :, p * LANES:(p + 1) * LANES] = kn
        ksb_ref[0, :, p * LANES:(p + 1) * LANES] = kn.astype(BF16)
    z = seg(6)
    vs_ref[0] = z
    vsb_ref[0] = z.astype(BF16)
    gs_ref[0] = seg(7)


def _project(x, norm_w, w_in_bf, cos_t, sin_t, qnw2, knw2, g2, tm):
    b, t, _ = x.shape
    grid = (b, t // tm)
    row = lambda i, j: (i, j, 0)
    const2 = lambda i, j: (0, 0)
    out_block = pl.BlockSpec((1, tm, WIDTH), row)
    f32_out = jax.ShapeDtypeStruct((b, t, WIDTH), F32)
    bf_out = jax.ShapeDtypeStruct((b, t, WIDTH), BF16)
    return pl.pallas_call(
        _proj_kernel,
        grid=grid,
        in_specs=[
            pl.BlockSpec((1, tm, D_MODEL), row),
            pl.BlockSpec((1, D_MODEL), const2),
            pl.BlockSpec((D_MODEL, N_SEG * WIDTH), const2),
            pl.BlockSpec((tm, LANES), lambda i, j: (j, 0)),
            pl.BlockSpec((tm, LANES), lambda i, j: (j, 0)),
            pl.BlockSpec((1, LANES), const2),
            pl.BlockSpec((1, LANES), const2),
            pl.BlockSpec((2 * LANES, LANES), const2),
        ],
        out_specs=[out_block] * 10,
        out_shape=[bf_out, bf_out, bf_out, f32_out, bf_out, f32_out, f32_out, f32_out,
                   bf_out, bf_out],
        compiler_params=pltpu.CompilerParams(
            dimension_semantics=("arbitrary", "arbitrary"),
            vmem_limit_bytes=VMEM_LIMIT),
        name="proj",
    )(x, norm_w, w_in_bf, cos_t, sin_t, qnw2, knw2, g2)


def _gate_norm_pair(o, g, w, g2):
    r = lax.rsqrt(_pair_head_mean(o * o, g2) + EPS)
    return ((o * r) * w) * _silu(g)


def _ret_kernel(q_ref, k_ref, v_ref, g_ref, km_ref, vm_ref, dmat_ref, cdec_ref, kdec_ref,
                kdecm_ref, gam_ref, bd_ref, w_ref, g2_ref, y_ref, sout_ref, state_ref):
    c = pl.program_id(1)
    bd = bd_ref[...] > 0.0
    g2 = g2_ref[...]
    lane = lax.broadcasted_iota(jnp.int32, (CHUNK, LANES), 1)
    low = lane < HEAD_DIM

    @pl.when(c == 0)
    def _():
        for p in range(N_PAIR):
            sl = slice(p * LANES, (p + 1) * LANES)
            kd = (km_ref[:, sl].astype(F32) * kdecm_ref[:, sl]).astype(BF16)
            state_ref[p] = jnp.where(bd, _tn_dot(kd, vm_ref[:, sl]), 0.0)

    for p in range(N_PAIR):
        sl = slice(p * LANES, (p + 1) * LANES)
        q = q_ref[0, :, sl]
        k = k_ref[0, :, sl]
        v = v_ref[0, :, sl]
        s = state_ref[p]
        inner = []
        for e in range(2):
            qm = jnp.where(low if e == 0 else ~low, q, jnp.zeros_like(q))
            sc = _nt_dot(qm, k) * dmat_ref[2 * p + e]
            inner.append(jnp.dot(sc.astype(BF16), v, preferred_element_type=F32))
        cross = jnp.dot(q, s.astype(BF16), preferred_element_type=F32) * cdec_ref[:, sl]
        o = jnp.where(low, inner[0], inner[1]) + cross
        kd = (k.astype(F32) * kdec_ref[:, sl]).astype(BF16)
        state_ref[p] = gam_ref[p] * s + jnp.where(bd, _tn_dot(kd, v), 0.0)
        y_ref[0, :, sl] = _gate_norm_pair(o, g_ref[0, :, sl], w_ref[:, sl], g2).astype(BF16)

    @pl.when(c == pl.num_programs(1) - 1)
    def _():
        sout_ref[0] = state_ref[...]


def _retention_prompt(qr, kr, vr, gr, kr_meta, vr_meta, tabs, w_onorm, g2):
    b, t, _ = qr.shape
    n_chunk = t // CHUNK
    row = lambda i, j: (i, j, 0)
    c2 = lambda i, j: (0, 0)
    c3 = lambda i, j: (0, 0, 0)
    blk = pl.BlockSpec((1, CHUNK, WIDTH), row)
    return pl.pallas_call(
        _ret_kernel,
        grid=(b, n_chunk),
        in_specs=[
            blk, blk, blk, blk,
            pl.BlockSpec((N_META, WIDTH), c2),
            pl.BlockSpec((N_META, WIDTH), c2),
            pl.BlockSpec((N_HEADS, CHUNK, CHUNK), c3),
            pl.BlockSpec((CHUNK, WIDTH), c2),
            pl.BlockSpec((CHUNK, WIDTH), c2),
            pl.BlockSpec((N_META, WIDTH), c2),
            pl.BlockSpec((N_PAIR, LANES, LANES), c3),
            pl.BlockSpec((LANES, LANES), c2),
            pl.BlockSpec((1, WIDTH), c2),
            pl.BlockSpec((2 * LANES, LANES), c2),
        ],
        out_specs=[blk, pl.BlockSpec((1, N_PAIR, LANES, LANES), lambda i, j: (i, 0, 0, 0))],
        out_shape=[jax.ShapeDtypeStruct((b, t, WIDTH), BF16),
                   jax.ShapeDtypeStruct((b, N_PAIR, LANES, LANES), F32)],
        scratch_shapes=[pltpu.VMEM((N_PAIR, LANES, LANES), F32)],
        compiler_params=pltpu.CompilerParams(
            dimension_semantics=("arbitrary", "arbitrary"),
            vmem_limit_bytes=VMEM_LIMIT),
        name="ret_prompt",
    )(qr, kr, vr, gr, kr_meta, vr_meta, tabs["dmat"], tabs["cdec"], tabs["kdec"],
      tabs["kdec_meta"], tabs["gam"], tabs["bd"], w_onorm, g2)


def _sb_block(qm_ref, k, v, bias_ref, tt, acc_ref, carry_ref, vis):
    for p in range(N_PAIR):
        sl = slice(p * LANES, (p + 1) * LANES)
        kp = k[:, sl]
        vp = v[:, sl]
        for e in range(2):
            h = 2 * p + e
            z = _nt_dot(qm_ref[h], kp) + bias_ref[h]
            ls, lk = _log_sigmoids(z)
            if vis is not None:
                lk = jnp.where(vis, lk, 0.0)
            cc = jnp.dot(_split_bf16(lk), tt, preferred_element_type=F32)
            a = jnp.exp(ls + cc[:, :LANES] + carry_ref[h])
            if vis is not None:
                a = jnp.where(vis, a, 0.0)
            acc_ref[h] += jnp.dot(a.astype(BF16), vp, preferred_element_type=F32)
            carry_ref[h] += cc[:, LANES:]


def _sb_kernel(bias_ref, q_ref, k_ref, v_ref, km_ref, vm_ref, g_ref, w_ref, tt_ref, g2_ref,
               y_ref, qm_ref, acc_ref, carry_ref):
    i = pl.program_id(1)
    tt = tt_ref[...]
    g2 = g2_ref[...]
    lane = lax.broadcasted_iota(jnp.int32, (CHUNK, LANES), 1)
    rowi = lax.broadcasted_iota(jnp.int32, (CHUNK, LANES), 0)
    low = lane < HEAD_DIM

    for p in range(N_PAIR):
        q = q_ref[0, :, p * LANES:(p + 1) * LANES]
        qm_ref[2 * p] = jnp.where(low, q, jnp.zeros_like(q))
        qm_ref[2 * p + 1] = jnp.where(low, jnp.zeros_like(q), q)
    acc_ref[...] = jnp.zeros_like(acc_ref)
    carry_ref[...] = jnp.zeros_like(carry_ref)

    base = pl.multiple_of(i * CHUNK, CHUNK)
    _sb_block(qm_ref, k_ref[0, pl.ds(base, CHUNK), :], v_ref[0, pl.ds(base, CHUNK), :],
              bias_ref, tt, acc_ref, carry_ref, lane < rowi)

    def body(jj, carry):
        off = pl.multiple_of((i - 1 - jj) * CHUNK, CHUNK)
        _sb_block(qm_ref, k_ref[0, pl.ds(off, CHUNK), :], v_ref[0, pl.ds(off, CHUNK), :],
                  bias_ref, tt, acc_ref, carry_ref, None)
        return carry

    lax.fori_loop(0, i, body, 0)

    _sb_block(qm_ref, km_ref[...], vm_ref[...], bias_ref, tt, acc_ref, carry_ref,
              lane < N_META)

    for p in range(N_PAIR):
        sl = slice(p * LANES, (p + 1) * LANES)
        o = jnp.where(low, acc_ref[2 * p], acc_ref[2 * p + 1])
        y_ref[0, :, sl] = _gate_norm_pair(o, g_ref[0, :, sl], w_ref[:, sl], g2).astype(BF16)


def _sb_prompt(sb_bias, qs, ksb, vsb, km_pad, vm_pad, gs, w_onorm, tt, g2):
    b, t, _ = qs.shape
    row = lambda i, j: (i, j, 0)
    full = lambda i, j: (i, 0, 0)
    c2 = lambda i, j: (0, 0)
    blk = pl.BlockSpec((1, CHUNK, WIDTH), row)
    return pl.pallas_call(
        _sb_kernel,
        grid=(b, t // CHUNK),
        in_specs=[
            pl.BlockSpec(memory_space=pltpu.SMEM),
            blk,
            pl.BlockSpec((1, t, WIDTH), full),
            pl.BlockSpec((1, t, WIDTH), full),
            pl.BlockSpec((CHUNK, WIDTH), c2),
            pl.BlockSpec((CHUNK, WIDTH), c2),
            blk,
            pl.BlockSpec((1, WIDTH), c2),
            pl.BlockSpec((2 * LANES, 2 * LANES), c2),
            pl.BlockSpec((2 * LANES, LANES), c2),
        ],
        out_specs=blk,
        out_shape=jax.ShapeDtypeStruct((b, t, WIDTH), BF16),
        scratch_shapes=[pltpu.VMEM((N_HEADS, CHUNK, LANES), BF16),
                        pltpu.VMEM((N_HEADS, CHUNK, LANES), F32),
                        pltpu.VMEM((N_HEADS, CHUNK, LANES), F32)],
        compiler_params=pltpu.CompilerParams(
            dimension_semantics=("arbitrary", "arbitrary"),
            vmem_limit_bytes=VMEM_LIMIT),
        name="sb_prompt",
    )(sb_bias, qs, ksb, vsb, km_pad, vm_pad, gs, w_onorm, tt, g2)


def _out_kernel(x_ref, yr_ref, ys_ref, w_ref, o_ref):
    o_ref[0] = (x_ref[0]
                + jnp.dot(yr_ref[0], w_ref[:WIDTH, :], preferred_element_type=F32)
                + jnp.dot(ys_ref[0], w_ref[WIDTH:, :], preferred_element_type=F32))


def _out_proj(x, yr, ys, w_out_bf, tm):
    b, t, _ = x.shape
    row = lambda i, j: (i, j, 0)
    return pl.pallas_call(
        _out_kernel,
        grid=(b, t // tm),
        in_specs=[
            pl.BlockSpec((1, tm, D_MODEL), row),
            pl.BlockSpec((1, tm, WIDTH), row),
            pl.BlockSpec((1, tm, WIDTH), row),
            pl.BlockSpec((2 * WIDTH, D_MODEL), lambda i, j: (0, 0)),
        ],
        out_specs=pl.BlockSpec((1, tm, D_MODEL), row),
        out_shape=jax.ShapeDtypeStruct(x.shape, F32),
        compiler_params=pltpu.CompilerParams(
            dimension_semantics=("arbitrary", "arbitrary"),
            vmem_limit_bytes=VMEM_LIMIT),
        name="out_proj",
    )(x, yr, ys, w_out_bf)


def _ret_sample_kernel(q_ref, k_ref, v_ref, g_ref, s_ref, dmat_ref, cdec_ref, kdec_ref,
                       gam_ref, w_ref, y_ref, sout_ref):
    n = q_ref.shape[0] * N_HEADS
    t = q_ref.shape[2]
    q = q_ref[...].reshape(n, t, HEAD_DIM)
    k = k_ref[...].reshape(n, t, HEAD_DIM)
    v = v_ref[...].reshape(n, t, HEAD_DIM)
    s = s_ref[...].reshape(n, HEAD_DIM, HEAD_DIM)
    sc = jnp.einsum("ntd,nsd->nts", q, k, preferred_element_type=F32) * dmat_ref[...]
    inner = jnp.einsum("nts,nse->nte", sc, v, preferred_element_type=F32)
    cross = jnp.einsum("ntd,nde->nte", q, s, preferred_element_type=F32) * cdec_ref[...]
    o = inner + cross
    upd = jnp.einsum("ntd,nte->nde", k * kdec_ref[...], v, preferred_element_type=F32)
    sout_ref[...] = (gam_ref[...] * s + upd).reshape(sout_ref.shape)
    r = lax.rsqrt(jnp.mean(o * o, axis=-1, keepdims=True) + EPS)
    g = g_ref[...].reshape(n, t, HEAD_DIM)
    y_ref[...] = (((o * r) * w_ref[...]) * _silu(g)).reshape(y_ref.shape)


def _retention_sample(q4, k4, v4, g4, state, tabs, w_heads, sblk):
    db, _, t, _ = q4.shape
    blk = pl.BlockSpec((sblk, N_HEADS, t, HEAD_DIM), lambda i: (i, 0, 0, 0))
    sblock = pl.BlockSpec((sblk, N_HEADS, HEAD_DIM, HEAD_DIM), lambda i: (i, 0, 0, 0))
    c3 = lambda i: (0, 0, 0)
    n = sblk * N_HEADS
    return pl.pallas_call(
        _ret_sample_kernel,
        grid=(db // sblk,),
        in_specs=[blk, blk, blk, blk, sblock,
                  pl.BlockSpec((n, t, t), c3),
                  pl.BlockSpec((n, t, 1), c3),
                  pl.BlockSpec((n, t, 1), c3),
                  pl.BlockSpec((n, 1, 1), c3),
                  pl.BlockSpec((n, 1, HEAD_DIM), c3)],
        out_specs=[blk, sblock],
        out_shape=[jax.ShapeDtypeStruct(q4.shape, F32),
                   jax.ShapeDtypeStruct(state.shape, F32)],
        compiler_params=pltpu.CompilerParams(
            dimension_semantics=("arbitrary",),
            vmem_limit_bytes=VMEM_LIMIT),
        name="ret_sample",
    )(q4, k4, v4, g4, state, tabs["dmat"], tabs["cdec"], tabs["kdec"], tabs["gam"], w_heads)


def _sb_sample_page(qbd, bias_row, k, v, tt2, carry, vis):
    z = jnp.dot(k, qbd, preferred_element_type=F32) + bias_row
    ls, lk = _log_sigmoids(z)
    if vis is not None:
        lk = jnp.where(vis, lk, 0.0)
    hi = lk.astype(BF16)
    lo = (lk - hi.astype(F32)).astype(BF16)
    cc = jnp.dot(tt2, jnp.concatenate([hi, lo], axis=0), preferred_element_type=F32)
    a = jnp.exp(ls + cc + carry)
    if vis is not None:
        a = jnp.where(vis, a, 0.0)
    part = _tn_dot(a.astype(BF16), v)
    return part, carry + cc[0:1, :] + lk[0:1, :]


def _sb_sample_kernel(pt_ref, qbd_ref, bias_ref, kn_ref, vn_ref, g_ref, w_ref, tt2_ref,
                      g2_ref, *rest, n_group):
    k_refs = rest[:n_group]
    v_refs = rest[n_group:2 * n_group]
    y_ref = rest[2 * n_group]
    acc_ref, carry_ref = rest[2 * n_group + 1:]
    j = pl.program_id(1)
    qbd = qbd_ref[0]
    bias_row = bias_ref[...]
    n_new = kn_ref.shape[1]
    t = y_ref.shape[1]

    @pl.when(j == 0)
    def _():
        rr = lax.broadcasted_iota(jnp.int32, (n_new, LANES), 0)
        col = lax.broadcasted_iota(jnp.int32, (n_new, LANES), 1)
        vis = rr < (col % t)
        tri = tt2_ref[:n_new, :n_new]
        part, carry = _sb_sample_page(qbd, bias_row, kn_ref[0].astype(BF16),
                                      vn_ref[0].astype(BF16),
                                      jnp.concatenate([tri, tri], axis=1),
                                      jnp.zeros((1, LANES), F32), vis)
        acc_ref[...] = part
        carry_ref[...] = carry

    tt2 = tt2_ref[...]
    carry = carry_ref[...]
    acc = acc_ref[...]
    for gidx in range(n_group):
        part, carry = _sb_sample_page(qbd, bias_row, k_refs[gidx][0].astype(BF16),
                                      v_refs[gidx][0].astype(BF16), tt2, carry, None)
        acc = acc + part
    acc_ref[...] = acc
    carry_ref[...] = carry

    @pl.when(j == pl.num_programs(1) - 1)
    def _():
        lane = lax.broadcasted_iota(jnp.int32, (t, WIDTH), 1)
        o = jnp.zeros((t, WIDTH), F32)
        for h in range(N_HEADS):
            o = o + jnp.where(lane // HEAD_DIM == h, acc_ref[h * t:(h + 1) * t, :], 0.0)
        g2 = g2_ref[...]
        for p in range(N_PAIR):
            sl = slice(p * LANES, (p + 1) * LANES)
            y_ref[0, :, sl] = _gate_norm_pair(o[:, sl], g_ref[0, :, sl], w_ref[:, sl],
                                              g2).astype(BF16)


def _sb_sample(page_table, qbd, bias_row, k_new, v_new, gs, w_onorm, tt2, g2, cache_k, cache_v,
               n_group):
    db, n_pages = page_table.shape
    t = gs.shape[1]
    n_new = k_new.shape[1]
    page = cache_k.shape[1]
    seq3 = lambda s, j, pt: (s, 0, 0)
    c2 = lambda s, j, pt: (0, 0)

    def page_map(gidx):
        return lambda s, j, pt: (pt[s, n_pages - 1 - (j * n_group + gidx)], 0, 0)

    page_specs = [pl.BlockSpec((1, page, WIDTH), page_map(gidx)) for gidx in range(n_group)]
    grid_spec = pltpu.PrefetchScalarGridSpec(
        num_scalar_prefetch=1,
        grid=(db, n_pages // n_group),
        in_specs=[
            pl.BlockSpec((1, WIDTH, LANES), seq3),
            pl.BlockSpec((1, LANES), c2),
            pl.BlockSpec((1, n_new, WIDTH), seq3),
            pl.BlockSpec((1, n_new, WIDTH), seq3),
            pl.BlockSpec((1, t, WIDTH), seq3),
            pl.BlockSpec((1, WIDTH), c2),
            pl.BlockSpec((page, 2 * page), c2),
            pl.BlockSpec((2 * LANES, LANES), c2),
        ] + page_specs + page_specs,
        out_specs=pl.BlockSpec((1, t, WIDTH), seq3),
        scratch_shapes=[pltpu.VMEM((LANES, WIDTH), F32), pltpu.VMEM((1, LANES), F32)],
    )
    return pl.pallas_call(
        functools.partial(_sb_sample_kernel, n_group=n_group),
        grid_spec=grid_spec,
        out_shape=jax.ShapeDtypeStruct((db, t, WIDTH), BF16),
        compiler_params=pltpu.CompilerParams(
            dimension_semantics=("arbitrary", "arbitrary"),
            vmem_limit_bytes=VMEM_LIMIT),
        name="sb_sample",
    )(page_table, qbd, bias_row, k_new, v_new, gs, w_onorm, tt2, g2,
      *([cache_k] * n_group), *([cache_v] * n_group))


def _rope_tables(pos):
    half = HEAD_DIM // 2
    inv = ROPE_BASE ** (-jnp.arange(half, dtype=F32) / half)
    ang = pos.astype(F32)[:, None] * inv[None, :]
    cos = jnp.tile(jnp.cos(ang), (1, LANES // half))
    sin = jnp.sin(ang)
    sin_signed = jnp.tile(jnp.concatenate([-sin, sin], axis=1), (1, LANES // HEAD_DIM))
    return cos, sin_signed


def _log_gamma():
    return jnp.log1p(-jnp.exp2(-5.0 - jnp.arange(N_HEADS, dtype=F32)))


def _head_lanes(x):
    return jnp.repeat(x, HEAD_DIM, axis=-1)


def _retention_tables_prompt():
    lg = _log_gamma()
    idx = jnp.arange(CHUNK, dtype=F32)
    rel = idx[:, None] - idx[None, :]
    dmat = jnp.where(rel[None] >= 0, jnp.exp(jnp.maximum(rel, 0.0)[None] * lg[:, None, None]), 0.0)
    cdec = _head_lanes(jnp.exp((idx + 1.0)[:, None] * lg[None, :]))
    kdec = _head_lanes(jnp.exp((CHUNK - 1.0 - idx)[:, None] * lg[None, :]))
    midx = jnp.arange(N_META, dtype=F32)
    kdec_meta = _head_lanes(jnp.exp((N_META - 1.0 - midx)[:, None] * lg[None, :]))
    head_of = jnp.arange(LANES) // HEAD_DIM
    bd = (head_of[:, None] == head_of[None, :]).astype(F32)
    gfull = jnp.exp(CHUNK * lg).reshape(N_PAIR, 2)
    gam = bd[None] * gfull[:, head_of][:, :, None]
    return {"dmat": dmat, "cdec": cdec, "kdec": kdec, "kdec_meta": kdec_meta, "gam": gam, "bd": bd}


def _retention_tables_sample(t, sblk):
    lg = _log_gamma()
    idx = jnp.arange(t, dtype=F32)
    rel = idx[:, None] - idx[None, :]
    dmat = jnp.where(rel[None] >= 0, jnp.exp(jnp.maximum(rel, 0.0)[None] * lg[:, None, None]), 0.0)
    cdec = jnp.exp((idx + 1.0)[None, :] * lg[:, None])[:, :, None]
    kdec = jnp.exp((t - 1.0 - idx)[None, :] * lg[:, None])[:, :, None]
    gam = jnp.exp(t * lg)[:, None, None]
    rep = lambda a: jnp.tile(a, (sblk, 1, 1))
    return {"dmat": rep(dmat), "cdec": rep(cdec), "kdec": rep(kdec), "gam": rep(gam)}


def _pair_mean_matrix():
    head_of = np.arange(LANES) // HEAD_DIM
    g = (head_of[:, None] == head_of[None, :]).astype(np.float32) / HEAD_DIM
    return jnp.asarray(np.concatenate([g, g], axis=0), dtype=BF16)


def _suffix_sum_matrix_lanes():
    i = np.arange(LANES)
    tri = (i[:, None] > i[None, :]).astype(np.float32)
    half = np.concatenate([tri, np.ones((LANES, LANES), np.float32)], axis=1)
    return jnp.asarray(np.concatenate([half, half], axis=0), dtype=BF16)


def _suffix_sum_matrix_rows(n):
    i = np.arange(n)
    tri = (i[None, :] > i[:, None]).astype(np.float32)
    return jnp.asarray(np.concatenate([tri, tri], axis=1), dtype=BF16)


def kernel(x_prompt, x_sample, cache_k, cache_v, state_ret, page_table, meta_tokens, norm_w, w_in,
           q_norm_w, k_norm_w, sb_bias, ret_onorm_w, sb_onorm_w, w_out):
    b, seq, _ = x_prompt.shape
    db, t, _ = x_sample.shape
    n_pool, page = cache_k.shape[1], cache_k.shape[2]
    n_pages = page_table.shape[1]
    past_len = n_pages * page

    w_in_bf = w_in[0].astype(BF16)
    w_out_bf = w_out[0].astype(BF16)
    nw = norm_w[0][None, :]
    qnw2 = jnp.tile(q_norm_w[0], LANES // HEAD_DIM)[None, :]
    knw2 = jnp.tile(k_norm_w[0], LANES // HEAD_DIM)[None, :]
    w_ret = ret_onorm_w[0][None, :]
    w_sb = sb_onorm_w[0][None, :]
    bias = sb_bias[0]
    g2 = _pair_mean_matrix()
    tt = _suffix_sum_matrix_lanes()

    cos_m, sin_m = _rope_tables(jnp.arange(N_META))
    cos_p, sin_p = _rope_tables(N_META + jnp.arange(seq))
    meta = _project(meta_tokens[None], nw, w_in_bf, cos_m, sin_m, qnw2, knw2, g2, N_META)
    _, kr_m, vr_m, _, _, ks_m, vs_m, _, ksb_m, vsb_m = meta
    qr, kr, vr, gr, qs, ks, vs, gs, ksb, vsb = _project(
        x_prompt, nw, w_in_bf, cos_p, sin_p, qnw2, knw2, g2, 256)

    tabs = _retention_tables_prompt()
    y_r, s_pairs = _retention_prompt(qr, kr, vr, gr, kr_m[0], vr_m[0], tabs, w_ret, g2)
    pad = ((0, CHUNK - N_META), (0, 0))
    y_s = _sb_prompt(bias, qs, ksb, vsb, jnp.pad(ksb_m[0], pad), jnp.pad(vsb_m[0], pad),
                     gs, w_sb, tt, g2)
    y_prompt = _out_proj(x_prompt, y_r, y_s, w_out_bf, 512)

    def with_meta(m, a):
        full = jnp.concatenate([jnp.broadcast_to(m, (b, N_META, WIDTH)), a], axis=1)
        return full.reshape(1, b, seq + N_META, N_HEADS, HEAD_DIM)

    new_k_prompt = with_meta(ks_m, ks)
    new_v_prompt = with_meta(vs_m, vs)
    sp = s_pairs.reshape(b, N_PAIR, 2, HEAD_DIM, 2, HEAD_DIM)
    new_state_prompt = jnp.stack([sp[:, :, 0, :, 0, :], sp[:, :, 1, :, 1, :]], axis=2)
    new_state_prompt = new_state_prompt.reshape(1, b, N_HEADS, HEAD_DIM, HEAD_DIM)

    n_tok = db * t
    cos_s, sin_s = _rope_tables(jnp.tile(past_len + jnp.arange(t), db))
    proj_s = _project(x_sample.reshape(1, n_tok, D_MODEL), nw, w_in_bf, cos_s, sin_s, qnw2, knw2,
                      g2, 256)
    qr_s, kr_s, vr_s, gr_s, qs_s, ks_s, vs_s, gs_s, _, _ = proj_s

    def heads_major(a):
        return a.reshape(db, t, N_HEADS, HEAD_DIM).transpose(0, 2, 1, 3).astype(F32)

    sblk = 8
    tabs_s = _retention_tables_sample(t, sblk)
    w_heads = jnp.tile(ret_onorm_w[0].reshape(N_HEADS, 1, HEAD_DIM), (sblk, 1, 1))
    y_r4, new_state_sample = _retention_sample(
        heads_major(qr_s), heads_major(kr_s), heads_major(vr_s), heads_major(gr_s),
        state_ret[0], tabs_s, w_heads, sblk)
    y_r_s = y_r4.transpose(0, 2, 1, 3).reshape(1, n_tok, WIDTH).astype(BF16)

    q4 = qs_s.reshape(db, t, N_HEADS, HEAD_DIM).transpose(0, 2, 3, 1)
    eye = jnp.eye(N_HEADS, dtype=BF16)
    qbd = (q4[:, :, :, None, :] * eye[None, :, None, :, None]).reshape(db, WIDTH, N_HEADS * t)
    qbd = jnp.pad(qbd, ((0, 0), (0, 0), (0, LANES - N_HEADS * t)))
    bias_row = jnp.pad(jnp.repeat(bias, t), (0, LANES - N_HEADS * t))[None, :]
    n_new = 16
    pad_new = ((0, 0), (0, n_new - t), (0, 0))
    k_new = jnp.pad(ks_s.reshape(db, t, WIDTH), pad_new)
    v_new = jnp.pad(vs_s.reshape(db, t, WIDTH), pad_new)
    tt2 = _suffix_sum_matrix_rows(page)
    y_s_s = _sb_sample(page_table, qbd, bias_row, k_new, v_new, gs_s.reshape(db, t, WIDTH), w_sb,
                       tt2, g2, cache_k[0].reshape(n_pool, page, WIDTH),
                       cache_v[0].reshape(n_pool, page, WIDTH), 8)
    y_sample = _out_proj(x_sample.reshape(1, n_tok, D_MODEL), y_r_s,
                         y_s_s.reshape(1, n_tok, WIDTH), w_out_bf, 512)
    y_sample = y_sample.reshape(db, t, D_MODEL)

    new_k_sample = ks_s.reshape(1, db, t, N_HEADS, HEAD_DIM)
    new_v_sample = vs_s.reshape(1, db, t, N_HEADS, HEAD_DIM)
    return (y_prompt, y_sample, new_k_prompt, new_v_prompt, new_state_prompt,
            new_k_sample, new_v_sample, new_state_sample[None])
```

```python
import functools

import numpy as np
import jax
import jax.numpy as jnp
from jax import lax
from jax.experimental import pallas as pl
from jax.experimental.pallas import tpu as pltpu

F32 = jnp.float32
BF16 = jnp.bfloat16

D_MODEL = 1024
HEAD_DIM = 64
N_HEADS = 8
WIDTH = N_HEADS * HEAD_DIM
N_SEG = 8
N_PAIR = N_HEADS // 2
LANES = 128
N_META = 16
CHUNK = 128
ROPE_BASE = 10000.0
EPS = 1e-6
QK_SCALE = HEAD_DIM ** -0.5
LOG2E = 1.4426950408889634
VMEM_LIMIT = 56 * 1024 * 1024


def _nt_dot(a, b):
    return lax.dot_general(a, b, (((1,), (1,)), ((), ())), preferred_element_type=F32)


def _tn_dot(a, b):
    return lax.dot_general(a, b, (((0,), (0,)), ((), ())), preferred_element_type=F32)


def _split_bf16(x):
    hi = x.astype(BF16)
    lo = (x - hi.astype(F32)).astype(BF16)
    return jnp.concatenate([hi, lo], axis=1)


def _pair_head_mean(sq, g2):
    return jnp.dot(_split_bf16(sq), g2, preferred_element_type=F32)


def _log2_sigmoids(z2):
    neg_abs = lax.bitcast_convert_type(
        lax.bitcast_convert_type(z2, jnp.uint32) | jnp.uint32(0x80000000), F32)
    t2 = jnp.log(1.0 + jnp.exp2(neg_abs)) * LOG2E
    ls2 = jnp.minimum(z2, 0.0) - t2
    return ls2, ls2 - z2


def _silu(g):
    return g * (1.0 / (1.0 + jnp.exp(-g)))


def _rope_pair(z, cos, sin_signed):
    lane = lax.broadcasted_iota(jnp.int32, z.shape, 1)
    first_half = (lane % HEAD_DIM) < (HEAD_DIM // 2)
    swapped = jnp.where(first_half,
                        pltpu.roll(z, LANES - HEAD_DIM // 2, axis=1),
                        pltpu.roll(z, HEAD_DIM // 2, axis=1))
    return z * cos + swapped * sin_signed


def _proj_kernel(x_ref, nw_ref, w_ref, cos_ref, sin_ref, qnw_ref, knw_ref, g2_ref,
                 qr_ref, kr_ref, vr_ref, gr_ref, qs_ref, ks_ref, vs_ref, gs_ref,
                 *attn_refs):
    x = x_ref[0]
    ms = jnp.mean(x * x, axis=-1, keepdims=True)
    xn = ((x * lax.rsqrt(ms + EPS)) * nw_ref[...]).astype(BF16)
    cos = cos_ref[...]
    sin = sin_ref[...]
    g2 = g2_ref[...]

    def seg(i):
        return jnp.dot(xn, w_ref[:, i * WIDTH:(i + 1) * WIDTH], preferred_element_type=F32)

    def pair(z, p):
        return z[:, p * LANES:(p + 1) * LANES]

    z = seg(0)
    for p in range(N_PAIR):
        qr_ref[0, :, p * LANES:(p + 1) * LANES] = _rope_pair(pair(z, p), cos, sin).astype(BF16)
    z = seg(1)
    for p in range(N_PAIR):
        kr_ref[0, :, p * LANES:(p + 1) * LANES] = (
            _rope_pair(pair(z, p), cos, sin) * QK_SCALE).astype(BF16)
    vr_ref[0] = seg(2).astype(BF16)
    gr_ref[0] = seg(3)
    z = seg(4)
    for p in range(N_PAIR):
        zp = pair(z, p)
        r = lax.rsqrt(_pair_head_mean(zp * zp, g2) + EPS)
        qs_ref[0, :, p * LANES:(p + 1) * LANES] = (
            (zp * r) * qnw_ref[...] * (QK_SCALE * LOG2E)).astype(BF16)
    z = seg(5)
    for p in range(N_PAIR):
        zp = pair(z, p)
        r = lax.rsqrt(_pair_head_mean(zp * zp, g2) + EPS)
        kn = (zp * r) * knw_ref[...]
        ks_ref[0, :, p * LANES:(p + 1) * LANES] = kn
        if attn_refs:
            kt = kn.T.astype(BF16)
            row = lax.broadcasted_iota(jnp.int32, (LANES, CHUNK), 0)
            zero = jnp.zeros((LANES, CHUNK), BF16)
            for u in range(kt.shape[1] // CHUNK):
                ku = kt[:, u * CHUNK:(u + 1) * CHUNK]
                attn_refs[0][0, p, u, :, :CHUNK] = jnp.where(row < HEAD_DIM, ku, zero)
                attn_refs[0][0, p, u, :, CHUNK:] = jnp.where(row < HEAD_DIM, zero, ku)
    z = seg(6)
    vs_ref[0] = z
    if attn_refs:
        lane = lax.broadcasted_iota(jnp.int32, (CHUNK, LANES), 1)
        zero = jnp.zeros((CHUNK, LANES), BF16)
        for p in range(N_PAIR):
            vb = pair(z, p).astype(BF16)
            for u in range(vb.shape[0] // CHUNK):
                vu = vb[u * CHUNK:(u + 1) * CHUNK, :]
                attn_refs[1][0, p, u, :CHUNK, :] = jnp.where(lane < HEAD_DIM, vu, zero)
                attn_refs[1][0, p, u, CHUNK:, :] = jnp.where(lane < HEAD_DIM, zero, vu)
    gs_ref[0] = seg(7)


def _project(x, norm_w, w_in_bf, cos_t, sin_t, qnw2, knw2, g2, tm, attn_blocks=False):
    b, t, _ = x.shape
    grid = (b, t // tm)
    row = lambda i, j: (i, j, 0)
    const2 = lambda i, j: (0, 0)
    out_block = pl.BlockSpec((1, tm, WIDTH), row)
    f32_out = jax.ShapeDtypeStruct((b, t, WIDTH), F32)
    bf_out = jax.ShapeDtypeStruct((b, t, WIDTH), BF16)
    out_specs = [out_block] * 8
    out_shape = [bf_out, bf_out, bf_out, f32_out, bf_out, f32_out, f32_out, f32_out]
    if attn_blocks:
        nb = tm // CHUNK
        blk5 = lambda i, j: (i, 0, j, 0, 0)
        out_specs += [pl.BlockSpec((1, N_PAIR, nb, LANES, 2 * CHUNK), blk5),
                      pl.BlockSpec((1, N_PAIR, nb, 2 * CHUNK, LANES), blk5)]
        out_shape += [jax.ShapeDtypeStruct((b, N_PAIR, t // CHUNK, LANES, 2 * CHUNK), BF16),
                      jax.ShapeDtypeStruct((b, N_PAIR, t // CHUNK, 2 * CHUNK, LANES), BF16)]
    return pl.pallas_call(
        _proj_kernel,
        grid=grid,
        in_specs=[
            pl.BlockSpec((1, tm, D_MODEL), row),
            pl.BlockSpec((1, D_MODEL), const2),
            pl.BlockSpec((D_MODEL, N_SEG * WIDTH), const2),
            pl.BlockSpec((tm, LANES), lambda i, j: (j, 0)),
            pl.BlockSpec((tm, LANES), lambda i, j: (j, 0)),
            pl.BlockSpec((1, LANES), const2),
            pl.BlockSpec((1, LANES), const2),
            pl.BlockSpec((2 * LANES, LANES), const2),
        ],
        out_specs=out_specs,
        out_shape=out_shape,
        compiler_params=pltpu.CompilerParams(
            dimension_semantics=("arbitrary", "arbitrary"),
            vmem_limit_bytes=VMEM_LIMIT),
        name="proj",
    )(x, norm_w, w_in_bf, cos_t, sin_t, qnw2, knw2, g2)


def _gate_norm_pair(o, g, w, g2):
    r = lax.rsqrt(_pair_head_mean(o * o, g2) + EPS)
    return ((o * r) * w) * _silu(g)


def _ret_kernel(q_ref, k_ref, v_ref, g_ref, km_ref, vm_ref, dmat_ref, cdec_ref, kdec_ref,
                kdecm_ref, gam_ref, bd_ref, w_ref, g2_ref, y_ref, sout_ref, state_ref):
    c = pl.program_id(1)
    bd = bd_ref[...] > 0.0
    g2 = g2_ref[...]
    lane = lax.broadcasted_iota(jnp.int32, (CHUNK, LANES), 1)
    low = lane < HEAD_DIM

    @pl.when(c == 0)
    def _():
        for p in range(N_PAIR):
            sl = slice(p * LANES, (p + 1) * LANES)
            kd = (km_ref[:, sl].astype(F32) * kdecm_ref[:, sl]).astype(BF16)
            state_ref[p] = jnp.where(bd, _tn_dot(kd, vm_ref[:, sl]), 0.0)

    for p in range(N_PAIR):
        sl = slice(p * LANES, (p + 1) * LANES)
        q = q_ref[0, :, sl]
        k = k_ref[0, :, sl]
        v = v_ref[0, :, sl]
        s = state_ref[p]
        inner = []
        for e in range(2):
            qm = jnp.where(low if e == 0 else ~low, q, jnp.zeros_like(q))
            sc = _nt_dot(qm, k) * dmat_ref[2 * p + e]
            inner.append(jnp.dot(sc.astype(BF16), v, preferred_element_type=F32))
        cross = jnp.dot(q, s.astype(BF16), preferred_element_type=F32) * cdec_ref[:, sl]
        o = jnp.where(low, inner[0], inner[1]) + cross
        kd = (k.astype(F32) * kdec_ref[:, sl]).astype(BF16)
        state_ref[p] = gam_ref[p] * s + jnp.where(bd, _tn_dot(kd, v), 0.0)
        y_ref[0, :, sl] = _gate_norm_pair(o, g_ref[0, :, sl], w_ref[:, sl], g2).astype(BF16)

    @pl.when(c == pl.num_programs(1) - 1)
    def _():
        sout_ref[0] = state_ref[...]


def _retention_prompt(qr, kr, vr, gr, kr_meta, vr_meta, tabs, w_onorm, g2):
    b, t, _ = qr.shape
    n_chunk = t // CHUNK
    row = lambda i, j: (i, j, 0)
    c2 = lambda i, j: (0, 0)
    c3 = lambda i, j: (0, 0, 0)
    blk = pl.BlockSpec((1, CHUNK, WIDTH), row)
    return pl.pallas_call(
        _ret_kernel,
        grid=(b, n_chunk),
        in_specs=[
            blk, blk, blk, blk,
            pl.BlockSpec((N_META, WIDTH), c2),
            pl.BlockSpec((N_META, WIDTH), c2),
            pl.BlockSpec((N_HEADS, CHUNK, CHUNK), c3),
            pl.BlockSpec((CHUNK, WIDTH), c2),
            pl.BlockSpec((CHUNK, WIDTH), c2),
            pl.BlockSpec((N_META, WIDTH), c2),
            pl.BlockSpec((N_PAIR, LANES, LANES), c3),
            pl.BlockSpec((LANES, LANES), c2),
            pl.BlockSpec((1, WIDTH), c2),
            pl.BlockSpec((2 * LANES, LANES), c2),
        ],
        out_specs=[blk, pl.BlockSpec((1, N_PAIR, LANES, LANES), lambda i, j: (i, 0, 0, 0))],
        out_shape=[jax.ShapeDtypeStruct((b, t, WIDTH), BF16),
                   jax.ShapeDtypeStruct((b, N_PAIR, LANES, LANES), F32)],
        scratch_shapes=[pltpu.VMEM((N_PAIR, LANES, LANES), F32)],
        compiler_params=pltpu.CompilerParams(
            dimension_semantics=("arbitrary", "arbitrary"),
            vmem_limit_bytes=VMEM_LIMIT),
        name="ret_prompt",
    )(qr, kr, vr, gr, kr_meta, vr_meta, tabs["dmat"], tabs["cdec"], tabs["kdec"],
      tabs["kdec_meta"], tabs["gam"], tabs["bd"], w_onorm, g2)


SB_Q = 2 * CHUNK


def _sb_step(q_ref, kt_of, v_of, bias_ref, tt, acc_ref, carry_ref, vis_of):
    vis = vis_head = None
    if vis_of is not None:
        def mask(width):
            key = lax.broadcasted_iota(jnp.int32, (SB_Q, width), 1) % LANES
            return vis_of(lax.broadcasted_iota(jnp.int32, (SB_Q, width), 0), key)
        vis, vis_head = mask(2 * LANES), mask(LANES)
    ls_all, lhs_all = [], []
    for p in range(N_PAIR):
        z = jnp.dot(q_ref[0, :, p * LANES:(p + 1) * LANES], kt_of(p),
                    preferred_element_type=F32) + bias_ref[p]
        ls, lk = _log2_sigmoids(z)
        if vis is not None:
            lk = jnp.where(vis, lk, 0.0)
        hi = lk.astype(BF16)
        lo = (lk - hi.astype(F32)).astype(BF16)
        ls_all.append(ls)
        for e in range(2):
            sl = slice(e * LANES, (e + 1) * LANES)
            lhs_all.append(jnp.concatenate([hi[:, sl], lo[:, sl]], axis=1))
    cc_all = [jnp.dot(lhs, tt, preferred_element_type=F32) for lhs in lhs_all]
    for p in range(N_PAIR):
        a_pair = []
        for e in range(2):
            h = 2 * p + e
            cc = cc_all[h]
            a = jnp.exp2(ls_all[p][:, e * LANES:(e + 1) * LANES] + cc[:, :LANES] + carry_ref[h])
            if vis is not None:
                a = jnp.where(vis_head, a, 0.0)
            a_pair.append(a.astype(BF16))
            carry_ref[h] += cc[:, LANES:]
        acc_ref[p] += jnp.dot(jnp.concatenate(a_pair, axis=1), v_of(p),
                              preferred_element_type=F32)


def _sb_kernel(q_ref, kt_ref, v_ref, ktm_ref, vm_ref, bias_ref, g_ref, w_ref, tt_ref, g2_ref,
               y_ref, acc_ref, carry_ref):
    i = pl.program_id(1)
    tt = tt_ref[...]
    g2 = g2_ref[...]

    acc_ref[...] = jnp.zeros_like(acc_ref)
    carry_ref[...] = jnp.zeros_like(carry_ref)

    for d in (1, 0):
        blk = 2 * i + d
        _sb_step(q_ref, lambda p: kt_ref[0, p, blk], lambda p: v_ref[0, p, blk],
                 bias_ref, tt, acc_ref, carry_ref,
                 lambda r, s, d=d: s < r - d * CHUNK)

    def body(jj, carry):
        blk = 2 * i - 1 - jj
        _sb_step(q_ref, lambda p: kt_ref[0, p, blk], lambda p: v_ref[0, p, blk],
                 bias_ref, tt, acc_ref, carry_ref, None)
        return carry

    lax.fori_loop(0, 2 * i, body, 0)

    _sb_step(q_ref, lambda p: ktm_ref[p], lambda p: vm_ref[p], bias_ref, tt, acc_ref,
             carry_ref, lambda r, s: s < N_META)

    for p in range(N_PAIR):
        sl = slice(p * LANES, (p + 1) * LANES)
        y_ref[0, :, sl] = _gate_norm_pair(acc_ref[p], g_ref[0, :, sl], w_ref[:, sl],
                                          g2).astype(BF16)


def _sb_prompt(qs, kt_bd, v_bd, ktm_bd, vm_bd, bias_pairs, gs, w_onorm, tt, g2):
    b, t, _ = qs.shape
    nblk = t // CHUNK
    row = lambda i, j: (i, j, 0)
    c2 = lambda i, j: (0, 0)
    c3 = lambda i, j: (0, 0, 0)
    batch5 = lambda i, j: (i, 0, 0, 0, 0)
    blk = pl.BlockSpec((1, SB_Q, WIDTH), row)
    once = pl.Buffered(1)
    return pl.pallas_call(
        _sb_kernel,
        grid=(b, t // SB_Q),
        in_specs=[
            blk,
            pl.BlockSpec((1, N_PAIR, nblk, LANES, 2 * CHUNK), batch5, pipeline_mode=once),
            pl.BlockSpec((1, N_PAIR, nblk, 2 * CHUNK, LANES), batch5, pipeline_mode=once),
            pl.BlockSpec((N_PAIR, LANES, 2 * CHUNK), c3),
            pl.BlockSpec((N_PAIR, 2 * CHUNK, LANES), c3),
            pl.BlockSpec((N_PAIR, 1, 2 * LANES), c3),
            blk,
            pl.BlockSpec((1, WIDTH), c2),
            pl.BlockSpec((2 * LANES, 2 * LANES), c2),
            pl.BlockSpec((2 * LANES, LANES), c2),
        ],
        out_specs=blk,
        out_shape=jax.ShapeDtypeStruct((b, t, WIDTH), BF16),
        scratch_shapes=[pltpu.VMEM((N_PAIR, SB_Q, LANES), F32),
                        pltpu.VMEM((N_HEADS, SB_Q, LANES), F32)],
        compiler_params=pltpu.CompilerParams(
            dimension_semantics=("arbitrary", "arbitrary"),
            vmem_limit_bytes=VMEM_LIMIT),
        name="sb_prompt",
    )(qs, kt_bd, v_bd, ktm_bd, vm_bd, bias_pairs, gs, w_onorm, tt, g2)


def _out_kernel(x_ref, yr_ref, ys_ref, w_ref, o_ref):
    o_ref[0] = (x_ref[0]
                + jnp.dot(yr_ref[0], w_ref[:WIDTH, :], preferred_element_type=F32)
                + jnp.dot(ys_ref[0], w_ref[WIDTH:, :], preferred_element_type=F32))


def _out_proj(x, yr, ys, w_out_bf, tm):
    b, t, _ = x.shape
    row = lambda i, j: (i, j, 0)
    return pl.pallas_call(
        _out_kernel,
        grid=(b, t // tm),
        in_specs=[
            pl.BlockSpec((1, tm, D_MODEL), row),
            pl.BlockSpec((1, tm, WIDTH), row),
            pl.BlockSpec((1, tm, WIDTH), row),
            pl.BlockSpec((2 * WIDTH, D_MODEL), lambda i, j: (0, 0)),
        ],
        out_specs=pl.BlockSpec((1, tm, D_MODEL), row),
        out_shape=jax.ShapeDtypeStruct(x.shape, F32),
        compiler_params=pltpu.CompilerParams(
            dimension_semantics=("arbitrary", "arbitrary"),
            vmem_limit_bytes=VMEM_LIMIT),
        name="out_proj",
    )(x, yr, ys, w_out_bf)


def _ret_sample_kernel(q_ref, k_ref, v_ref, g_ref, s_ref, dmat_ref, cdec_ref, kdec_ref,
                       gam_ref, w_ref, y_ref, sout_ref):
    n = q_ref.shape[0] * N_HEADS
    t = q_ref.shape[2]
    q = q_ref[...].reshape(n, t, HEAD_DIM)
    k = k_ref[...].reshape(n, t, HEAD_DIM)
    v = v_ref[...].reshape(n, t, HEAD_DIM)
    s = s_ref[...].reshape(n, HEAD_DIM, HEAD_DIM)
    sc = jnp.einsum("ntd,nsd->nts", q, k, preferred_element_type=F32) * dmat_ref[...]
    inner = jnp.einsum("nts,nse->nte", sc, v, preferred_element_type=F32)
    cross = jnp.einsum("ntd,nde->nte", q, s, preferred_element_type=F32) * cdec_ref[...]
    o = inner + cross
    upd = jnp.einsum("ntd,nte->nde", k * kdec_ref[...], v, preferred_element_type=F32)
    sout_ref[...] = (gam_ref[...] * s + upd).reshape(sout_ref.shape)
    r = lax.rsqrt(jnp.mean(o * o, axis=-1, keepdims=True) + EPS)
    g = g_ref[...].reshape(n, t, HEAD_DIM)
    y_ref[...] = (((o * r) * w_ref[...]) * _silu(g)).reshape(y_ref.shape)


def _retention_sample(q4, k4, v4, g4, state, tabs, w_heads, sblk):
    db, _, t, _ = q4.shape
    blk = pl.BlockSpec((sblk, N_HEADS, t, HEAD_DIM), lambda i: (i, 0, 0, 0))
    sblock = pl.BlockSpec((sblk, N_HEADS, HEAD_DIM, HEAD_DIM), lambda i: (i, 0, 0, 0))
    c3 = lambda i: (0, 0, 0)
    n = sblk * N_HEADS
    return pl.pallas_call(
        _ret_sample_kernel,
        grid=(db // sblk,),
        in_specs=[blk, blk, blk, blk, sblock,
                  pl.BlockSpec((n, t, t), c3),
                  pl.BlockSpec((n, t, 1), c3),
                  pl.BlockSpec((n, t, 1), c3),
                  pl.BlockSpec((n, 1, 1), c3),
                  pl.BlockSpec((n, 1, HEAD_DIM), c3)],
        out_specs=[blk, sblock],
        out_shape=[jax.ShapeDtypeStruct(q4.shape, F32),
                   jax.ShapeDtypeStruct(state.shape, F32)],
        compiler_params=pltpu.CompilerParams(
            dimension_semantics=("arbitrary",),
            vmem_limit_bytes=VMEM_LIMIT),
        name="ret_sample",
    )(q4, k4, v4, g4, state, tabs["dmat"], tabs["cdec"], tabs["kdec"], tabs["gam"], w_heads)


def _sample_pages(qblk, bias, kts, vts, tt, carry, vis):
    n = len(kts)
    rows = qblk.shape[0]
    z = jnp.dot(qblk, jnp.concatenate(kts, axis=1), preferred_element_type=F32)
    z = z + jnp.concatenate([bias] * n, axis=1)
    ls, lk = _log2_sigmoids(z)
    if vis is not None:
        lk = jnp.where(vis, lk, 0.0)
    lk_rows = jnp.concatenate([lk[:, g * LANES:(g + 1) * LANES] for g in range(n)], axis=0)
    cc = jnp.dot(_split_bf16(lk_rows), tt, preferred_element_type=F32)
    a_all = []
    for g in range(n):
        ccg = cc[g * rows:(g + 1) * rows]
        a = jnp.exp2(ls[:, g * LANES:(g + 1) * LANES] + ccg[:, :LANES] + carry)
        if vis is not None:
            a = jnp.where(vis, a, 0.0)
        a_all.append(a.astype(BF16))
        carry = carry + ccg[:, LANES:]
    a_cat = jnp.concatenate(a_all, axis=1)
    a_pad = jnp.concatenate([a_cat, jnp.zeros((LANES - rows, a_cat.shape[1]), BF16)], axis=0)
    return _nt_dot(jnp.concatenate(vts, axis=1), a_pad), carry


def _sb_sample_kernel(pt_ref, qblk_ref, bias_ref, ktn_ref, vtn_ref, g_ref, w_ref, tt_ref,
                      g2_ref, *rest, n_group):
    k_refs = rest[:n_group]
    v_refs = rest[n_group:2 * n_group]
    y_ref = rest[2 * n_group]
    acc_ref, carry_ref = rest[2 * n_group + 1:]
    j = pl.program_id(1)
    qblk = qblk_ref[0]
    bias = bias_ref[...]
    tt = tt_ref[...]
    t = y_ref.shape[1]
    rows = qblk.shape[0]

    @pl.when(j == 0)
    def _():
        rr = lax.broadcasted_iota(jnp.int32, (rows, LANES), 0)
        col = lax.broadcasted_iota(jnp.int32, (rows, LANES), 1)
        part, carry = _sample_pages(qblk, bias, [ktn_ref[0]], [vtn_ref[0]], tt,
                                    jnp.zeros((rows, LANES), F32), col < (rr % t))
        acc_ref[...] = part
        carry_ref[...] = carry

    part, carry = _sample_pages(qblk, bias,
                                [r[0].astype(BF16) for r in k_refs],
                                [r[0].astype(BF16) for r in v_refs],
                                tt, carry_ref[...], None)
    acc_ref[...] += part
    carry_ref[...] = carry

    @pl.when(j == pl.num_programs(1) - 1)
    def _():
        acc = acc_ref[...].T
        lane = lax.broadcasted_iota(jnp.int32, (t, WIDTH), 1)
        o = jnp.zeros((t, WIDTH), F32)
        for h in range(N_HEADS):
            o = o + jnp.where(lane // HEAD_DIM == h, acc[h * t:(h + 1) * t, :], 0.0)
        g2 = g2_ref[...]
        for p in range(N_PAIR):
            sl = slice(p * LANES, (p + 1) * LANES)
            y_ref[0, :, sl] = _gate_norm_pair(o[:, sl], g_ref[0, :, sl], w_ref[:, sl],
                                              g2).astype(BF16)


def _sb_sample(page_table, qblk, bias, kt_new, vt_new, gs, w_onorm, tt, g2, cache_kt, cache_vt,
               n_group):
    db, n_pages = page_table.shape
    t = gs.shape[1]
    rows = qblk.shape[1]
    page = cache_kt.shape[2]
    seq3 = lambda s, j, pt: (s, 0, 0)
    c2 = lambda s, j, pt: (0, 0)

    def page_map(gidx):
        return lambda s, j, pt: (pt[s, n_pages - 1 - (j * n_group + gidx)], 0, 0)

    page_specs = [pl.BlockSpec((1, WIDTH, page), page_map(gidx)) for gidx in range(n_group)]
    grid_spec = pltpu.PrefetchScalarGridSpec(
        num_scalar_prefetch=1,
        grid=(db, n_pages // n_group),
        in_specs=[
            pl.BlockSpec((1, rows, WIDTH), seq3),
            pl.BlockSpec((rows, LANES), c2),
            pl.BlockSpec((1, WIDTH, LANES), seq3),
            pl.BlockSpec((1, WIDTH, LANES), seq3),
            pl.BlockSpec((1, t, WIDTH), seq3),
            pl.BlockSpec((1, WIDTH), c2),
            pl.BlockSpec((2 * LANES, 2 * LANES), c2),
            pl.BlockSpec((2 * LANES, LANES), c2),
        ] + page_specs + page_specs,
        out_specs=pl.BlockSpec((1, t, WIDTH), seq3),
        scratch_shapes=[pltpu.VMEM((WIDTH, LANES), F32), pltpu.VMEM((rows, LANES), F32)],
    )
    return pl.pallas_call(
        functools.partial(_sb_sample_kernel, n_group=n_group),
        grid_spec=grid_spec,
        out_shape=jax.ShapeDtypeStruct((db, t, WIDTH), BF16),
        compiler_params=pltpu.CompilerParams(
            dimension_semantics=("arbitrary", "arbitrary"),
            vmem_limit_bytes=VMEM_LIMIT),
        name="sb_sample",
    )(page_table, qblk, bias, kt_new, vt_new, gs, w_onorm, tt, g2,
      *([cache_kt] * n_group), *([cache_vt] * n_group))


def _rope_tables(pos):
    half = HEAD_DIM // 2
    inv = ROPE_BASE ** (-jnp.arange(half, dtype=F32) / half)
    ang = pos.astype(F32)[:, None] * inv[None, :]
    cos = jnp.tile(jnp.cos(ang), (1, LANES // half))
    sin = jnp.sin(ang)
    sin_signed = jnp.tile(jnp.concatenate([-sin, sin], axis=1), (1, LANES // HEAD_DIM))
    return cos, sin_signed


def _log_gamma():
    return jnp.log1p(-jnp.exp2(-5.0 - jnp.arange(N_HEADS, dtype=F32)))


def _head_lanes(x):
    return jnp.repeat(x, HEAD_DIM, axis=-1)


def _retention_tables_prompt():
    lg = _log_gamma()
    idx = jnp.arange(CHUNK, dtype=F32)
    rel = idx[:, None] - idx[None, :]
    dmat = jnp.where(rel[None] >= 0, jnp.exp(jnp.maximum(rel, 0.0)[None] * lg[:, None, None]), 0.0)
    cdec = _head_lanes(jnp.exp((idx + 1.0)[:, None] * lg[None, :]))
    kdec = _head_lanes(jnp.exp((CHUNK - 1.0 - idx)[:, None] * lg[None, :]))
    midx = jnp.arange(N_META, dtype=F32)
    kdec_meta = _head_lanes(jnp.exp((N_META - 1.0 - midx)[:, None] * lg[None, :]))
    head_of = jnp.arange(LANES) // HEAD_DIM
    bd = (head_of[:, None] == head_of[None, :]).astype(F32)
    gfull = jnp.exp(CHUNK * lg).reshape(N_PAIR, 2)
    gam = bd[None] * gfull[:, head_of][:, :, None]
    return {"dmat": dmat, "cdec": cdec, "kdec": kdec, "kdec_meta": kdec_meta, "gam": gam, "bd": bd}


def _retention_tables_sample(t, sblk):
    lg = _log_gamma()
    idx = jnp.arange(t, dtype=F32)
    rel = idx[:, None] - idx[None, :]
    dmat = jnp.where(rel[None] >= 0, jnp.exp(jnp.maximum(rel, 0.0)[None] * lg[:, None, None]), 0.0)
    cdec = jnp.exp((idx + 1.0)[None, :] * lg[:, None])[:, :, None]
    kdec = jnp.exp((t - 1.0 - idx)[None, :] * lg[:, None])[:, :, None]
    gam = jnp.exp(t * lg)[:, None, None]
    rep = lambda a: jnp.tile(a, (sblk, 1, 1))
    return {"dmat": rep(dmat), "cdec": rep(cdec), "kdec": rep(kdec), "gam": rep(gam)}


def _pair_mean_matrix():
    head_of = np.arange(LANES) // HEAD_DIM
    g = (head_of[:, None] == head_of[None, :]).astype(np.float32) / HEAD_DIM
    return jnp.asarray(np.concatenate([g, g], axis=0), dtype=BF16)


def _suffix_sum_matrix_lanes():
    i = np.arange(LANES)
    tri = (i[:, None] > i[None, :]).astype(np.float32)
    half = np.concatenate([tri, np.ones((LANES, LANES), np.float32)], axis=1)
    return jnp.asarray(np.concatenate([half, half], axis=0), dtype=BF16)


def kernel(x_prompt, x_sample, cache_k, cache_v, state_ret, page_table, meta_tokens, norm_w, w_in,
           q_norm_w, k_norm_w, sb_bias, ret_onorm_w, sb_onorm_w, w_out):
    b, seq, _ = x_prompt.shape
    db, t, _ = x_sample.shape
    n_pool, page = cache_k.shape[1], cache_k.shape[2]
    n_pages = page_table.shape[1]
    past_len = n_pages * page

    w_in_bf = w_in[0].astype(BF16)
    w_out_bf = w_out[0].astype(BF16)
    nw = norm_w[0][None, :]
    qnw2 = jnp.tile(q_norm_w[0], LANES // HEAD_DIM)[None, :]
    knw2 = jnp.tile(k_norm_w[0], LANES // HEAD_DIM)[None, :]
    w_ret = ret_onorm_w[0][None, :]
    w_sb = sb_onorm_w[0][None, :]
    bias = sb_bias[0] * LOG2E
    g2 = _pair_mean_matrix()
    tt = _suffix_sum_matrix_lanes()

    cos_m, sin_m = _rope_tables(jnp.arange(N_META))
    cos_p, sin_p = _rope_tables(N_META + jnp.arange(seq))
    _, kr_m, vr_m, _, _, ks_m, vs_m, _ = _project(
        meta_tokens[None], nw, w_in_bf, cos_m, sin_m, qnw2, knw2, g2, N_META)
    qr, kr, vr, gr, qs, ks, vs, gs, kt_bd, v_bd = _project(
        x_prompt, nw, w_in_bf, cos_p, sin_p, qnw2, knw2, g2, 256, attn_blocks=True)

    tabs = _retention_tables_prompt()
    y_r, s_pairs = _retention_prompt(qr, kr, vr, gr, kr_m[0], vr_m[0], tabs, w_ret, g2)

    pad = ((0, CHUNK - N_META), (0, 0))
    km = jnp.pad(ks_m[0], pad).astype(BF16).reshape(CHUNK, N_PAIR, 2, HEAD_DIM)
    vm = jnp.pad(vs_m[0], pad).astype(BF16).reshape(CHUNK, N_PAIR, 2, HEAD_DIM)
    eye2 = jnp.eye(2, dtype=BF16)
    ktm_bd = (km.transpose(1, 2, 3, 0)[:, :, :, None, :] * eye2[None, :, None, :, None]
              ).reshape(N_PAIR, LANES, 2 * CHUNK)
    vm_bd = (vm.transpose(1, 0, 2, 3)[:, None, :, :, :] * eye2[None, :, None, :, None]
             ).reshape(N_PAIR, 2 * CHUNK, LANES)
    bias_pairs = jnp.repeat(bias.reshape(N_PAIR, 1, 2), LANES, axis=2)
    y_s = _sb_prompt(qs, kt_bd, v_bd, ktm_bd, vm_bd, bias_pairs, gs, w_sb, tt, g2)
    y_prompt = _out_proj(x_prompt, y_r, y_s, w_out_bf, 512)

    def with_meta(m, a):
        full = jnp.concatenate([jnp.broadcast_to(m, (b, N_META, WIDTH)), a], axis=1)
        return full.reshape(1, b, seq + N_META, N_HEADS, HEAD_DIM)

    new_k_prompt = with_meta(ks_m, ks)
    new_v_prompt = with_meta(vs_m, vs)
    sp = s_pairs.reshape(b, N_PAIR, 2, HEAD_DIM, 2, HEAD_DIM)
    new_state_prompt = jnp.stack([sp[:, :, 0, :, 0, :], sp[:, :, 1, :, 1, :]], axis=2)
    new_state_prompt = new_state_prompt.reshape(1, b, N_HEADS, HEAD_DIM, HEAD_DIM)

    n_tok = db * t
    cos_s, sin_s = _rope_tables(jnp.tile(past_len + jnp.arange(t), db))
    qr_s, kr_s, vr_s, gr_s, qs_s, ks_s, vs_s, gs_s = _project(
        x_sample.reshape(1, n_tok, D_MODEL), nw, w_in_bf, cos_s, sin_s, qnw2, knw2, g2, 256)

    def heads_major(a):
        return a.reshape(db, t, N_HEADS, HEAD_DIM).transpose(0, 2, 1, 3).astype(F32)

    sblk = 8
    tabs_s = _retention_tables_sample(t, sblk)
    w_heads = jnp.tile(ret_onorm_w[0].reshape(N_HEADS, 1, HEAD_DIM), (sblk, 1, 1))
    y_r4, new_state_sample = _retention_sample(
        heads_major(qr_s), heads_major(kr_s), heads_major(vr_s), heads_major(gr_s),
        state_ret[0], tabs_s, w_heads, sblk)
    y_r_s = y_r4.transpose(0, 2, 1, 3).reshape(1, n_tok, WIDTH).astype(BF16)

    q4 = qs_s.reshape(db, t, N_HEADS, HEAD_DIM).transpose(0, 2, 1, 3)
    eye = jnp.eye(N_HEADS, dtype=BF16)
    qblk = (q4[:, :, :, None, :] * eye[None, :, None, :, None]).reshape(db, N_HEADS * t, WIDTH)
    bias_rows = jnp.broadcast_to(jnp.repeat(bias, t)[:, None], (N_HEADS * t, LANES))

    def new_tokens_t(a):
        at = a.reshape(db, t, WIDTH).transpose(0, 2, 1).astype(BF16)
        return jnp.pad(at, ((0, 0), (0, 0), (0, LANES - t)))

    def pages_t(c):
        return c[0].transpose(0, 2, 3, 1).reshape(n_pool, WIDTH, page)

    y_s_s = _sb_sample(page_table, qblk, bias_rows, new_tokens_t(ks_s[0]), new_tokens_t(vs_s[0]),
                       gs_s.reshape(db, t, WIDTH), w_sb, tt, g2, pages_t(cache_k),
                       pages_t(cache_v), 16)
    y_sample = _out_proj(x_sample.reshape(1, n_tok, D_MODEL), y_r_s,
                         y_s_s.reshape(1, n_tok, WIDTH), w_out_bf, 512)
    y_sample = y_sample.reshape(db, t, D_MODEL)

    new_k_sample = ks_s.reshape(1, db, t, N_HEADS, HEAD_DIM)
    new_v_sample = vs_s.reshape(1, db, t, N_HEADS, HEAD_DIM)
    return (y_prompt, y_sample, new_k_prompt, new_v_prompt, new_state_prompt,
            new_k_sample, new_v_sample, new_state_sample[None])
```

```python
import functools

import numpy as np
import jax
import jax.numpy as jnp
from jax import lax
from jax.experimental import pallas as pl
from jax.experimental.pallas import tpu as pltpu

F32 = jnp.float32
BF16 = jnp.bfloat16

D_MODEL = 1024
HEAD_DIM = 64
N_HEADS = 8
WIDTH = N_HEADS * HEAD_DIM
N_SEG = 8
N_PAIR = N_HEADS // 2
LANES = 128
N_META = 16
CHUNK = 128
ROPE_BASE = 10000.0
EPS = 1e-6
QK_SCALE = HEAD_DIM ** -0.5
LOG2E = 1.4426950408889634
EXP2_MAX = 126.0
VMEM_LIMIT = 56 * 1024 * 1024


def _nt_dot(a, b):
    return lax.dot_general(a, b, (((1,), (1,)), ((), ())), preferred_element_type=F32)


def _tn_dot(a, b):
    return lax.dot_general(a, b, (((0,), (0,)), ((), ())), preferred_element_type=F32)


def _split_bf16(x):
    hi = x.astype(BF16)
    lo = (x - hi.astype(F32)).astype(BF16)
    return jnp.concatenate([hi, lo], axis=1)


def _pair_head_mean(sq, g2):
    return jnp.dot(_split_bf16(sq), g2, preferred_element_type=F32)


def _log2_sigmoids(z2):
    zc = jnp.minimum(z2, EXP2_MAX)
    lk2 = jnp.log(1.0 + jnp.exp2(zc)) * (-LOG2E)
    return lk2 + zc, lk2


def _silu(g):
    return g * (1.0 / (1.0 + jnp.exp(-g)))


def _rope_pair(z, cos, sin_signed):
    lane = lax.broadcasted_iota(jnp.int32, z.shape, 1)
    first_half = (lane % HEAD_DIM) < (HEAD_DIM // 2)
    swapped = jnp.where(first_half,
                        pltpu.roll(z, LANES - HEAD_DIM // 2, axis=1),
                        pltpu.roll(z, HEAD_DIM // 2, axis=1))
    return z * cos + swapped * sin_signed


def _proj_kernel(x_ref, nw_ref, w_ref, cos_ref, sin_ref, qnw_ref, knw_ref, g2_ref,
                 qr_ref, kr_ref, vr_ref, gr_ref, qs_ref, ks_ref, vs_ref, gs_ref,
                 *attn_refs):
    x = x_ref[0]
    ms = jnp.mean(x * x, axis=-1, keepdims=True)
    xn = ((x * lax.rsqrt(ms + EPS)) * nw_ref[...]).astype(BF16)
    cos = cos_ref[...]
    sin = sin_ref[...]
    g2 = g2_ref[...]

    def seg(i):
        return jnp.dot(xn, w_ref[:, i * WIDTH:(i + 1) * WIDTH], preferred_element_type=F32)

    def pair(z, p):
        return z[:, p * LANES:(p + 1) * LANES]

    z = seg(0)
    for p in range(N_PAIR):
        qr_ref[0, :, p * LANES:(p + 1) * LANES] = _rope_pair(pair(z, p), cos, sin).astype(BF16)
    z = seg(1)
    for p in range(N_PAIR):
        kr_ref[0, :, p * LANES:(p + 1) * LANES] = (
            _rope_pair(pair(z, p), cos, sin) * QK_SCALE).astype(BF16)
    vr_ref[0] = seg(2).astype(BF16)
    gr_ref[0] = seg(3)
    z = seg(4)
    for p in range(N_PAIR):
        zp = pair(z, p)
        r = lax.rsqrt(_pair_head_mean(zp * zp, g2) + EPS)
        qs_ref[0, :, p * LANES:(p + 1) * LANES] = (
            (zp * r) * qnw_ref[...] * (QK_SCALE * LOG2E)).astype(BF16)
    z = seg(5)
    for p in range(N_PAIR):
        zp = pair(z, p)
        r = lax.rsqrt(_pair_head_mean(zp * zp, g2) + EPS)
        kn = (zp * r) * knw_ref[...]
        ks_ref[0, :, p * LANES:(p + 1) * LANES] = kn
        if attn_refs:
            kt = kn.T.astype(BF16)
            row = lax.broadcasted_iota(jnp.int32, (LANES, CHUNK), 0)
            zero = jnp.zeros((LANES, CHUNK), BF16)
            for u in range(kt.shape[1] // CHUNK):
                ku = kt[:, u * CHUNK:(u + 1) * CHUNK]
                attn_refs[0][0, p, u, :, :CHUNK] = jnp.where(row < HEAD_DIM, ku, zero)
                attn_refs[0][0, p, u, :, CHUNK:] = jnp.where(row < HEAD_DIM, zero, ku)
    z = seg(6)
    vs_ref[0] = z
    if attn_refs:
        lane = lax.broadcasted_iota(jnp.int32, (CHUNK, LANES), 1)
        zero = jnp.zeros((CHUNK, LANES), BF16)
        for p in range(N_PAIR):
            vb = pair(z, p).astype(BF16)
            for u in range(vb.shape[0] // CHUNK):
                vu = vb[u * CHUNK:(u + 1) * CHUNK, :]
                attn_refs[1][0, p, u, :CHUNK, :] = jnp.where(lane < HEAD_DIM, vu, zero)
                attn_refs[1][0, p, u, CHUNK:, :] = jnp.where(lane < HEAD_DIM, zero, vu)
    gs_ref[0] = seg(7)


def _project(x, norm_w, w_in_bf, cos_t, sin_t, qnw2, knw2, g2, tm, attn_blocks=False):
    b, t, _ = x.shape
    grid = (b, t // tm)
    row = lambda i, j: (i, j, 0)
    const2 = lambda i, j: (0, 0)
    out_block = pl.BlockSpec((1, tm, WIDTH), row)
    f32_out = jax.ShapeDtypeStruct((b, t, WIDTH), F32)
    bf_out = jax.ShapeDtypeStruct((b, t, WIDTH), BF16)
    out_specs = [out_block] * 8
    out_shape = [bf_out, bf_out, bf_out, f32_out, bf_out, f32_out, f32_out, f32_out]
    if attn_blocks:
        nb = tm // CHUNK
        blk5 = lambda i, j: (i, 0, j, 0, 0)
        out_specs += [pl.BlockSpec((1, N_PAIR, nb, LANES, 2 * CHUNK), blk5),
                      pl.BlockSpec((1, N_PAIR, nb, 2 * CHUNK, LANES), blk5)]
        out_shape += [jax.ShapeDtypeStruct((b, N_PAIR, t // CHUNK, LANES, 2 * CHUNK), BF16),
                      jax.ShapeDtypeStruct((b, N_PAIR, t // CHUNK, 2 * CHUNK, LANES), BF16)]
    return pl.pallas_call(
        _proj_kernel,
        grid=grid,
        in_specs=[
            pl.BlockSpec((1, tm, D_MODEL), row),
            pl.BlockSpec((1, D_MODEL), const2),
            pl.BlockSpec((D_MODEL, N_SEG * WIDTH), const2),
            pl.BlockSpec((tm, LANES), lambda i, j: (j, 0)),
            pl.BlockSpec((tm, LANES), lambda i, j: (j, 0)),
            pl.BlockSpec((1, LANES), const2),
            pl.BlockSpec((1, LANES), const2),
            pl.BlockSpec((2 * LANES, LANES), const2),
        ],
        out_specs=out_specs,
        out_shape=out_shape,
        compiler_params=pltpu.CompilerParams(
            dimension_semantics=("arbitrary", "arbitrary"),
            vmem_limit_bytes=VMEM_LIMIT),
        name="proj",
    )(x, norm_w, w_in_bf, cos_t, sin_t, qnw2, knw2, g2)


RET_CHUNKS = 2


def _gate_norm_pair(o, g, w, g2):
    r = lax.rsqrt(_pair_head_mean(o * o, g2) + EPS)
    return ((o * r) * w) * _silu(g)


def _ret_kernel(q_ref, k_ref, v_ref, g_ref, km_ref, vm_ref, dmat_ref, cdec_ref, kdec_ref,
                kdecm_ref, gam_ref, bd_ref, w_ref, g2_ref, y_ref, sout_ref, state_ref):
    c = pl.program_id(1)
    bd = bd_ref[...] > 0.0
    g2 = g2_ref[...]
    lane = lax.broadcasted_iota(jnp.int32, (CHUNK, LANES), 1)
    low = lane < HEAD_DIM

    @pl.when(c == 0)
    def _():
        for p in range(N_PAIR):
            sl = slice(p * LANES, (p + 1) * LANES)
            kd = (km_ref[:, sl].astype(F32) * kdecm_ref[:, sl]).astype(BF16)
            state_ref[p] = jnp.where(bd, _tn_dot(kd, vm_ref[:, sl]), 0.0)

    zero = jnp.zeros((CHUNK, LANES), BF16)
    for u in range(RET_CHUNKS):
        rows = slice(u * CHUNK, (u + 1) * CHUNK)
        vs, sc, cross, upd, old = [], [], [], [], []
        for p in range(N_PAIR):
            sl = slice(p * LANES, (p + 1) * LANES)
            q = q_ref[0, rows, sl]
            k = k_ref[0, rows, sl]
            v = v_ref[0, rows, sl]
            s = state_ref[p]
            sc.append([_nt_dot(jnp.where(low, q, zero), k), _nt_dot(jnp.where(low, zero, q), k)])
            cross.append(jnp.dot(q, s.astype(BF16), preferred_element_type=F32))
            kd = (k.astype(F32) * kdec_ref[:, sl]).astype(BF16)
            upd.append(_tn_dot(kd, v))
            vs.append(v)
            old.append(s)
        for p in range(N_PAIR):
            state_ref[p] = gam_ref[p] * old[p] + jnp.where(bd, upd[p], 0.0)
            sc[p] = [(sc[p][e] * dmat_ref[2 * p + e]).astype(BF16) for e in range(2)]
        inner = [[jnp.dot(sc[p][e], vs[p], preferred_element_type=F32) for e in range(2)]
                 for p in range(N_PAIR)]
        for p in range(N_PAIR):
            sl = slice(p * LANES, (p + 1) * LANES)
            o = jnp.where(low, inner[p][0], inner[p][1]) + cross[p] * cdec_ref[:, sl]
            y_ref[0, rows, sl] = _gate_norm_pair(o, g_ref[0, rows, sl], w_ref[:, sl],
                                                 g2).astype(BF16)

    @pl.when(c == pl.num_programs(1) - 1)
    def _():
        sout_ref[0] = state_ref[...]


def _retention_prompt(qr, kr, vr, gr, kr_meta, vr_meta, tabs, w_onorm, g2):
    b, t, _ = qr.shape
    row = lambda i, j: (i, j, 0)
    c2 = lambda i, j: (0, 0)
    c3 = lambda i, j: (0, 0, 0)
    blk = pl.BlockSpec((1, RET_CHUNKS * CHUNK, WIDTH), row)
    return pl.pallas_call(
        _ret_kernel,
        grid=(b, t // (RET_CHUNKS * CHUNK)),
        in_specs=[
            blk, blk, blk, blk,
            pl.BlockSpec((N_META, WIDTH), c2),
            pl.BlockSpec((N_META, WIDTH), c2),
            pl.BlockSpec((N_HEADS, CHUNK, CHUNK), c3),
            pl.BlockSpec((CHUNK, WIDTH), c2),
            pl.BlockSpec((CHUNK, WIDTH), c2),
            pl.BlockSpec((N_META, WIDTH), c2),
            pl.BlockSpec((N_PAIR, LANES, LANES), c3),
            pl.BlockSpec((LANES, LANES), c2),
            pl.BlockSpec((1, WIDTH), c2),
            pl.BlockSpec((2 * LANES, LANES), c2),
        ],
        out_specs=[blk, pl.BlockSpec((1, N_PAIR, LANES, LANES), lambda i, j: (i, 0, 0, 0))],
        out_shape=[jax.ShapeDtypeStruct((b, t, WIDTH), BF16),
                   jax.ShapeDtypeStruct((b, N_PAIR, LANES, LANES), F32)],
        scratch_shapes=[pltpu.VMEM((N_PAIR, LANES, LANES), F32)],
        compiler_params=pltpu.CompilerParams(
            dimension_semantics=("arbitrary", "arbitrary"),
            vmem_limit_bytes=VMEM_LIMIT),
        name="ret_prompt",
    )(qr, kr, vr, gr, kr_meta, vr_meta, tabs["dmat"], tabs["cdec"], tabs["kdec"],
      tabs["kdec_meta"], tabs["gam"], tabs["bd"], w_onorm, g2)


SB_Q = 2 * CHUNK
SB_KB = 2


def _sb_step(q_ref, blocks, bias_ref, tt, acc_ref, carry_ref):
    def mask(vis_of, width):
        key = lax.broadcasted_iota(jnp.int32, (SB_Q, width), 1) % LANES
        return vis_of(lax.broadcasted_iota(jnp.int32, (SB_Q, width), 0), key)

    ls_all, lkb_all, tot_all = [], [], []
    for kt_of, _, vis_of in blocks:
        for p in range(N_PAIR):
            z = jnp.dot(q_ref[0, :, p * LANES:(p + 1) * LANES], kt_of(p),
                        preferred_element_type=F32) + bias_ref[p]
            ls, lk = _log2_sigmoids(z)
            if vis_of is not None:
                lk = jnp.where(mask(vis_of, 2 * LANES), lk, 0.0)
            ls_all.append(ls)
            lkb_all.append(lk.astype(BF16))
            tot_all.append([jnp.sum(lk[:, e * LANES:(e + 1) * LANES], axis=1, keepdims=True)
                            for e in range(2)])
    cc_all = [jnp.dot(lkb, tt, preferred_element_type=F32) for lkb in lkb_all]
    for p in range(N_PAIR):
        a_run = []
        for n, (_, _, vis_of) in enumerate(blocks):
            s = ls_all[n * N_PAIR + p] + cc_all[n * N_PAIR + p]
            for e in range(2):
                h = 2 * p + e
                a = jnp.exp2(s[:, e * LANES:(e + 1) * LANES] + carry_ref[h])
                if vis_of is not None:
                    a = jnp.where(mask(vis_of, LANES), a, 0.0)
                a_run.append(a.astype(BF16))
                carry_ref[h] += tot_all[n * N_PAIR + p][e]
        acc_ref[p] += jnp.dot(jnp.concatenate(a_run, axis=1),
                              jnp.concatenate([v_of(p) for _, v_of, _ in blocks], axis=0),
                              preferred_element_type=F32)


def _sb_kernel(q_ref, kt_ref, v_ref, ktm_ref, vm_ref, bias_ref, g_ref, w_ref, tt_ref, g2_ref,
               y_ref, acc_ref, carry_ref):
    i = pl.program_id(1)
    tt = tt_ref[...]
    g2 = g2_ref[...]

    acc_ref[...] = jnp.zeros_like(acc_ref)
    carry_ref[...] = jnp.zeros_like(carry_ref)

    def cached(blk, vis_of):
        return (lambda p: kt_ref[0, p, blk], lambda p: v_ref[0, p, blk], vis_of)

    _sb_step(q_ref, [cached(2 * i + d, lambda r, s, d=d: s < r - d * CHUNK) for d in (1, 0)],
             bias_ref, tt, acc_ref, carry_ref)

    def body(jj, carry):
        newest = 2 * i - 1 - SB_KB * jj
        _sb_step(q_ref, [cached(newest - n, None) for n in range(SB_KB)],
                 bias_ref, tt, acc_ref, carry_ref)
        return carry

    lax.fori_loop(0, (2 * i) // SB_KB, body, 0)

    _sb_step(q_ref, [(lambda p: ktm_ref[p], lambda p: vm_ref[p], lambda r, s: s < N_META)],
             bias_ref, tt, acc_ref, carry_ref)

    for p in range(N_PAIR):
        sl = slice(p * LANES, (p + 1) * LANES)
        y_ref[0, :, sl] = _gate_norm_pair(acc_ref[p], g_ref[0, :, sl], w_ref[:, sl],
                                          g2).astype(BF16)


def _sb_prompt(qs, kt_bd, v_bd, ktm_bd, vm_bd, bias_pairs, gs, w_onorm, tt, g2):
    b, t, _ = qs.shape
    nblk = t // CHUNK
    row = lambda i, j: (i, j, 0)
    c2 = lambda i, j: (0, 0)
    c3 = lambda i, j: (0, 0, 0)
    batch5 = lambda i, j: (i, 0, 0, 0, 0)
    blk = pl.BlockSpec((1, SB_Q, WIDTH), row)
    once = pl.Buffered(1)
    return pl.pallas_call(
        _sb_kernel,
        grid=(b, t // SB_Q),
        in_specs=[
            blk,
            pl.BlockSpec((1, N_PAIR, nblk, LANES, 2 * CHUNK), batch5, pipeline_mode=once),
            pl.BlockSpec((1, N_PAIR, nblk, 2 * CHUNK, LANES), batch5, pipeline_mode=once),
            pl.BlockSpec((N_PAIR, LANES, 2 * CHUNK), c3),
            pl.BlockSpec((N_PAIR, 2 * CHUNK, LANES), c3),
            pl.BlockSpec((N_PAIR, 1, 2 * LANES), c3),
            blk,
            pl.BlockSpec((1, WIDTH), c2),
            pl.BlockSpec((2 * LANES, 2 * LANES), c2),
            pl.BlockSpec((2 * LANES, LANES), c2),
        ],
        out_specs=blk,
        out_shape=jax.ShapeDtypeStruct((b, t, WIDTH), BF16),
        scratch_shapes=[pltpu.VMEM((N_PAIR, SB_Q, LANES), F32),
                        pltpu.VMEM((N_HEADS, SB_Q, LANES), F32)],
        compiler_params=pltpu.CompilerParams(
            dimension_semantics=("arbitrary", "arbitrary"),
            vmem_limit_bytes=VMEM_LIMIT),
        name="sb_prompt",
    )(qs, kt_bd, v_bd, ktm_bd, vm_bd, bias_pairs, gs, w_onorm, tt, g2)


def _out_kernel(x_ref, yr_ref, ys_ref, w_ref, o_ref):
    o_ref[0] = (x_ref[0]
                + jnp.dot(yr_ref[0], w_ref[:WIDTH, :], preferred_element_type=F32)
                + jnp.dot(ys_ref[0], w_ref[WIDTH:, :], preferred_element_type=F32))


def _out_proj(x, yr, ys, w_out_bf, tm):
    b, t, _ = x.shape
    row = lambda i, j: (i, j, 0)
    return pl.pallas_call(
        _out_kernel,
        grid=(b, t // tm),
        in_specs=[
            pl.BlockSpec((1, tm, D_MODEL), row),
            pl.BlockSpec((1, tm, WIDTH), row),
            pl.BlockSpec((1, tm, WIDTH), row),
            pl.BlockSpec((2 * WIDTH, D_MODEL), lambda i, j: (0, 0)),
        ],
        out_specs=pl.BlockSpec((1, tm, D_MODEL), row),
        out_shape=jax.ShapeDtypeStruct(x.shape, F32),
        compiler_params=pltpu.CompilerParams(
            dimension_semantics=("arbitrary", "arbitrary"),
            vmem_limit_bytes=VMEM_LIMIT),
        name="out_proj",
    )(x, yr, ys, w_out_bf)


def _ret_sample_kernel(q_ref, k_ref, v_ref, g_ref, s_ref, dmat_ref, cdec_ref, kdec_ref,
                       gam_ref, w_ref, y_ref, sout_ref):
    n = q_ref.shape[0] * N_HEADS
    t = q_ref.shape[2]
    q = q_ref[...].reshape(n, t, HEAD_DIM)
    k = k_ref[...].reshape(n, t, HEAD_DIM)
    v = v_ref[...].reshape(n, t, HEAD_DIM)
    s = s_ref[...].reshape(n, HEAD_DIM, HEAD_DIM)
    sc = jnp.einsum("ntd,nsd->nts", q, k, preferred_element_type=F32) * dmat_ref[...]
    inner = jnp.einsum("nts,nse->nte", sc, v, preferred_element_type=F32)
    cross = jnp.einsum("ntd,nde->nte", q, s, preferred_element_type=F32) * cdec_ref[...]
    o = inner + cross
    upd = jnp.einsum("ntd,nte->nde", k * kdec_ref[...], v, preferred_element_type=F32)
    sout_ref[...] = (gam_ref[...] * s + upd).reshape(sout_ref.shape)
    r = lax.rsqrt(jnp.mean(o * o, axis=-1, keepdims=True) + EPS)
    g = g_ref[...].reshape(n, t, HEAD_DIM)
    y_ref[...] = (((o * r) * w_ref[...]) * _silu(g)).reshape(y_ref.shape)


def _retention_sample(q4, k4, v4, g4, state, tabs, w_heads, sblk):
    db, _, t, _ = q4.shape
    blk = pl.BlockSpec((sblk, N_HEADS, t, HEAD_DIM), lambda i: (i, 0, 0, 0))
    sblock = pl.BlockSpec((sblk, N_HEADS, HEAD_DIM, HEAD_DIM), lambda i: (i, 0, 0, 0))
    c3 = lambda i: (0, 0, 0)
    n = sblk * N_HEADS
    return pl.pallas_call(
        _ret_sample_kernel,
        grid=(db // sblk,),
        in_specs=[blk, blk, blk, blk, sblock,
                  pl.BlockSpec((n, t, t), c3),
                  pl.BlockSpec((n, t, 1), c3),
                  pl.BlockSpec((n, t, 1), c3),
                  pl.BlockSpec((n, 1, 1), c3),
                  pl.BlockSpec((n, 1, HEAD_DIM), c3)],
        out_specs=[blk, sblock],
        out_shape=[jax.ShapeDtypeStruct(q4.shape, F32),
                   jax.ShapeDtypeStruct(state.shape, F32)],
        compiler_params=pltpu.CompilerParams(
            dimension_semantics=("arbitrary",),
            vmem_limit_bytes=VMEM_LIMIT),
        name="ret_sample",
    )(q4, k4, v4, g4, state, tabs["dmat"], tabs["cdec"], tabs["kdec"], tabs["gam"], w_heads)


def _sample_pages(qblk, bias, kts, vts, tt, carry, vis):
    n = len(kts)
    rows = qblk.shape[0]
    z = jnp.dot(qblk, jnp.concatenate(kts, axis=1), preferred_element_type=F32)
    z = z + jnp.concatenate([bias] * n, axis=1)
    ls, lk = _log2_sigmoids(z)
    if vis is not None:
        lk = jnp.where(vis, lk, 0.0)
    lk_rows = jnp.concatenate([lk[:, g * LANES:(g + 1) * LANES] for g in range(n)], axis=0)
    cc = jnp.dot(_split_bf16(lk_rows), tt, preferred_element_type=F32)
    a_all = []
    for g in range(n):
        ccg = cc[g * rows:(g + 1) * rows]
        a = jnp.exp2(ls[:, g * LANES:(g + 1) * LANES] + ccg[:, :LANES] + carry)
        if vis is not None:
            a = jnp.where(vis, a, 0.0)
        a_all.append(a.astype(BF16))
        carry = carry + ccg[:, LANES:]
    a_cat = jnp.concatenate(a_all, axis=1)
    a_pad = jnp.concatenate([a_cat, jnp.zeros((LANES - rows, a_cat.shape[1]), BF16)], axis=0)
    return _nt_dot(jnp.concatenate(vts, axis=1), a_pad), carry


def _sb_sample_kernel(pt_ref, qblk_ref, bias_ref, ktn_ref, vtn_ref, g_ref, w_ref, tt_ref,
                      g2_ref, *rest, n_group):
    k_refs = rest[:n_group]
    v_refs = rest[n_group:2 * n_group]
    y_ref = rest[2 * n_group]
    acc_ref, carry_ref = rest[2 * n_group + 1:]
    j = pl.program_id(1)
    qblk = qblk_ref[0]
    bias = bias_ref[...]
    tt = tt_ref[...]
    t = y_ref.shape[1]
    rows = qblk.shape[0]

    @pl.when(j == 0)
    def _():
        rr = lax.broadcasted_iota(jnp.int32, (rows, LANES), 0)
        col = lax.broadcasted_iota(jnp.int32, (rows, LANES), 1)
        part, carry = _sample_pages(qblk, bias, [ktn_ref[0]], [vtn_ref[0]], tt,
                                    jnp.zeros((rows, LANES), F32), col < (rr % t))
        acc_ref[...] = part
        carry_ref[...] = carry

    part, carry = _sample_pages(qblk, bias,
                                [r[0].astype(BF16) for r in k_refs],
                                [r[0].astype(BF16) for r in v_refs],
                                tt, carry_ref[...], None)
    acc_ref[...] += part
    carry_ref[...] = carry

    @pl.when(j == pl.num_programs(1) - 1)
    def _():
        acc = acc_ref[...].T
        lane = lax.broadcasted_iota(jnp.int32, (t, WIDTH), 1)
        o = jnp.zeros((t, WIDTH), F32)
        for h in range(N_HEADS):
            o = o + jnp.where(lane // HEAD_DIM == h, acc[h * t:(h + 1) * t, :], 0.0)
        g2 = g2_ref[...]
        for p in range(N_PAIR):
            sl = slice(p * LANES, (p + 1) * LANES)
            y_ref[0, :, sl] = _gate_norm_pair(o[:, sl], g_ref[0, :, sl], w_ref[:, sl],
                                              g2).astype(BF16)


def _sb_sample(page_table, qblk, bias, kt_new, vt_new, gs, w_onorm, tt, g2, cache_kt, cache_vt,
               n_group):
    db, n_pages = page_table.shape
    t = gs.shape[1]
    rows = qblk.shape[1]
    page = cache_kt.shape[2]
    seq3 = lambda s, j, pt: (s, 0, 0)
    c2 = lambda s, j, pt: (0, 0)

    def page_map(gidx):
        return lambda s, j, pt: (pt[s, n_pages - 1 - (j * n_group + gidx)], 0, 0)

    page_specs = [pl.BlockSpec((1, WIDTH, page), page_map(gidx)) for gidx in range(n_group)]
    grid_spec = pltpu.PrefetchScalarGridSpec(
        num_scalar_prefetch=1,
        grid=(db, n_pages // n_group),
        in_specs=[
            pl.BlockSpec((1, rows, WIDTH), seq3),
            pl.BlockSpec((rows, LANES), c2),
            pl.BlockSpec((1, WIDTH, LANES), seq3),
            pl.BlockSpec((1, WIDTH, LANES), seq3),
            pl.BlockSpec((1, t, WIDTH), seq3),
            pl.BlockSpec((1, WIDTH), c2),
            pl.BlockSpec((2 * LANES, 2 * LANES), c2),
            pl.BlockSpec((2 * LANES, LANES), c2),
        ] + page_specs + page_specs,
        out_specs=pl.BlockSpec((1, t, WIDTH), seq3),
        scratch_shapes=[pltpu.VMEM((WIDTH, LANES), F32), pltpu.VMEM((rows, LANES), F32)],
    )
    return pl.pallas_call(
        functools.partial(_sb_sample_kernel, n_group=n_group),
        grid_spec=grid_spec,
        out_shape=jax.ShapeDtypeStruct((db, t, WIDTH), BF16),
        compiler_params=pltpu.CompilerParams(
            dimension_semantics=("arbitrary", "arbitrary"),
            vmem_limit_bytes=VMEM_LIMIT),
        name="sb_sample",
    )(page_table, qblk, bias, kt_new, vt_new, gs, w_onorm, tt, g2,
      *([cache_kt] * n_group), *([cache_vt] * n_group))


def _rope_tables(pos):
    half = HEAD_DIM // 2
    inv = ROPE_BASE ** (-jnp.arange(half, dtype=F32) / half)
    ang = pos.astype(F32)[:, None] * inv[None, :]
    cos = jnp.tile(jnp.cos(ang), (1, LANES // half))
    sin = jnp.sin(ang)
    sin_signed = jnp.tile(jnp.concatenate([-sin, sin], axis=1), (1, LANES // HEAD_DIM))
    return cos, sin_signed


def _log_gamma():
    return jnp.log1p(-jnp.exp2(-5.0 - jnp.arange(N_HEADS, dtype=F32)))


def _head_lanes(x):
    return jnp.repeat(x, HEAD_DIM, axis=-1)


def _retention_tables_prompt():
    lg = _log_gamma()
    idx = jnp.arange(CHUNK, dtype=F32)
    rel = idx[:, None] - idx[None, :]
    dmat = jnp.where(rel[None] >= 0, jnp.exp(jnp.maximum(rel, 0.0)[None] * lg[:, None, None]), 0.0)
    cdec = _head_lanes(jnp.exp((idx + 1.0)[:, None] * lg[None, :]))
    kdec = _head_lanes(jnp.exp((CHUNK - 1.0 - idx)[:, None] * lg[None, :]))
    midx = jnp.arange(N_META, dtype=F32)
    kdec_meta = _head_lanes(jnp.exp((N_META - 1.0 - midx)[:, None] * lg[None, :]))
    head_of = jnp.arange(LANES) // HEAD_DIM
    bd = (head_of[:, None] == head_of[None, :]).astype(F32)
    gfull = jnp.exp(CHUNK * lg).reshape(N_PAIR, 2)
    gam = bd[None] * gfull[:, head_of][:, :, None]
    return {"dmat": dmat, "cdec": cdec, "kdec": kdec, "kdec_meta": kdec_meta, "gam": gam, "bd": bd}


def _retention_tables_sample(t, sblk):
    lg = _log_gamma()
    idx = jnp.arange(t, dtype=F32)
    rel = idx[:, None] - idx[None, :]
    dmat = jnp.where(rel[None] >= 0, jnp.exp(jnp.maximum(rel, 0.0)[None] * lg[:, None, None]), 0.0)
    cdec = jnp.exp((idx + 1.0)[None, :] * lg[:, None])[:, :, None]
    kdec = jnp.exp((t - 1.0 - idx)[None, :] * lg[:, None])[:, :, None]
    gam = jnp.exp(t * lg)[:, None, None]
    rep = lambda a: jnp.tile(a, (sblk, 1, 1))
    return {"dmat": rep(dmat), "cdec": rep(cdec), "kdec": rep(kdec), "gam": rep(gam)}


def _pair_mean_matrix():
    head_of = np.arange(LANES) // HEAD_DIM
    g = (head_of[:, None] == head_of[None, :]).astype(np.float32) / HEAD_DIM
    return jnp.asarray(np.concatenate([g, g], axis=0), dtype=BF16)


def _pair_suffix_sum_matrix():
    i = np.arange(LANES)
    tri = (i[:, None] > i[None, :]).astype(np.float32)
    zero = np.zeros((LANES, LANES), np.float32)
    return jnp.asarray(np.block([[tri, zero], [zero, tri]]), dtype=BF16)


def _suffix_sum_matrix_lanes():
    i = np.arange(LANES)
    tri = (i[:, None] > i[None, :]).astype(np.float32)
    half = np.concatenate([tri, np.ones((LANES, LANES), np.float32)], axis=1)
    return jnp.asarray(np.concatenate([half, half], axis=0), dtype=BF16)


def kernel(x_prompt, x_sample, cache_k, cache_v, state_ret, page_table, meta_tokens, norm_w, w_in,
           q_norm_w, k_norm_w, sb_bias, ret_onorm_w, sb_onorm_w, w_out):
    b, seq, _ = x_prompt.shape
    db, t, _ = x_sample.shape
    n_pool, page = cache_k.shape[1], cache_k.shape[2]
    n_pages = page_table.shape[1]
    past_len = n_pages * page

    w_in_bf = w_in[0].astype(BF16)
    w_out_bf = w_out[0].astype(BF16)
    nw = norm_w[0][None, :]
    qnw2 = jnp.tile(q_norm_w[0], LANES // HEAD_DIM)[None, :]
    knw2 = jnp.tile(k_norm_w[0], LANES // HEAD_DIM)[None, :]
    w_ret = ret_onorm_w[0][None, :]
    w_sb = sb_onorm_w[0][None, :]
    bias = sb_bias[0] * LOG2E
    g2 = _pair_mean_matrix()
    tt = _suffix_sum_matrix_lanes()

    cos_m, sin_m = _rope_tables(jnp.arange(N_META))
    cos_p, sin_p = _rope_tables(N_META + jnp.arange(seq))
    _, kr_m, vr_m, _, _, ks_m, vs_m, _ = _project(
        meta_tokens[None], nw, w_in_bf, cos_m, sin_m, qnw2, knw2, g2, N_META)
    qr, kr, vr, gr, qs, ks, vs, gs, kt_bd, v_bd = _project(
        x_prompt, nw, w_in_bf, cos_p, sin_p, qnw2, knw2, g2, 256, attn_blocks=True)

    tabs = _retention_tables_prompt()
    y_r, s_pairs = _retention_prompt(qr, kr, vr, gr, kr_m[0], vr_m[0], tabs, w_ret, g2)

    pad = ((0, CHUNK - N_META), (0, 0))
    km = jnp.pad(ks_m[0], pad).astype(BF16).reshape(CHUNK, N_PAIR, 2, HEAD_DIM)
    vm = jnp.pad(vs_m[0], pad).astype(BF16).reshape(CHUNK, N_PAIR, 2, HEAD_DIM)
    eye2 = jnp.eye(2, dtype=BF16)
    ktm_bd = (km.transpose(1, 2, 3, 0)[:, :, :, None, :] * eye2[None, :, None, :, None]
              ).reshape(N_PAIR, LANES, 2 * CHUNK)
    vm_bd = (vm.transpose(1, 0, 2, 3)[:, None, :, :, :] * eye2[None, :, None, :, None]
             ).reshape(N_PAIR, 2 * CHUNK, LANES)
    bias_pairs = jnp.repeat(bias.reshape(N_PAIR, 1, 2), LANES, axis=2)
    y_s = _sb_prompt(qs, kt_bd, v_bd, ktm_bd, vm_bd, bias_pairs, gs, w_sb,
                     _pair_suffix_sum_matrix(), g2)
    y_prompt = _out_proj(x_prompt, y_r, y_s, w_out_bf, 512)

    def with_meta(m, a):
        full = jnp.concatenate([jnp.broadcast_to(m, (b, N_META, WIDTH)), a], axis=1)
        return full.reshape(1, b, seq + N_META, N_HEADS, HEAD_DIM)

    new_k_prompt = with_meta(ks_m, ks)
    new_v_prompt = with_meta(vs_m, vs)
    sp = s_pairs.reshape(b, N_PAIR, 2, HEAD_DIM, 2, HEAD_DIM)
    new_state_prompt = jnp.stack([sp[:, :, 0, :, 0, :], sp[:, :, 1, :, 1, :]], axis=2)
    new_state_prompt = new_state_prompt.reshape(1, b, N_HEADS, HEAD_DIM, HEAD_DIM)

    n_tok = db * t
    cos_s, sin_s = _rope_tables(jnp.tile(past_len + jnp.arange(t), db))
    qr_s, kr_s, vr_s, gr_s, qs_s, ks_s, vs_s, gs_s = _project(
        x_sample.reshape(1, n_tok, D_MODEL), nw, w_in_bf, cos_s, sin_s, qnw2, knw2, g2, 256)

    def heads_major(a):
        return a.reshape(db, t, N_HEADS, HEAD_DIM).transpose(0, 2, 1, 3).astype(F32)

    sblk = 8
    tabs_s = _retention_tables_sample(t, sblk)
    w_heads = jnp.tile(ret_onorm_w[0].reshape(N_HEADS, 1, HEAD_DIM), (sblk, 1, 1))
    y_r4, new_state_sample = _retention_sample(
        heads_major(qr_s), heads_major(kr_s), heads_major(vr_s), heads_major(gr_s),
        state_ret[0], tabs_s, w_heads, sblk)
    y_r_s = y_r4.transpose(0, 2, 1, 3).reshape(1, n_tok, WIDTH).astype(BF16)

    q4 = qs_s.reshape(db, t, N_HEADS, HEAD_DIM).transpose(0, 2, 1, 3)
    eye = jnp.eye(N_HEADS, dtype=BF16)
    qblk = (q4[:, :, :, None, :] * eye[None, :, None, :, None]).reshape(db, N_HEADS * t, WIDTH)
    bias_rows = jnp.broadcast_to(jnp.repeat(bias, t)[:, None], (N_HEADS * t, LANES))

    def new_tokens_t(a):
        at = a.reshape(db, t, WIDTH).transpose(0, 2, 1).astype(BF16)
        return jnp.pad(at, ((0, 0), (0, 0), (0, LANES - t)))

    def pages_t(c):
        return c[0].transpose(0, 2, 3, 1).reshape(n_pool, WIDTH, page)

    y_s_s = _sb_sample(page_table, qblk, bias_rows, new_tokens_t(ks_s[0]), new_tokens_t(vs_s[0]),
                       gs_s.reshape(db, t, WIDTH), w_sb, tt, g2, pages_t(cache_k),
                       pages_t(cache_v), 16)
    y_sample = _out_proj(x_sample.reshape(1, n_tok, D_MODEL), y_r_s,
                         y_s_s.reshape(1, n_tok, WIDTH), w_out_bf, 512)
    y_sample = y_sample.reshape(db, t, D_MODEL)

    new_k_sample = ks_s.reshape(1, db, t, N_HEADS, HEAD_DIM)
    new_v_sample = vs_s.reshape(1, db, t, N_HEADS, HEAD_DIM)
    return (y_prompt, y_sample, new_k_prompt, new_v_prompt, new_state_prompt,
            new_k_sample, new_v_sample, new_state_sample[None])
```

```python
import functools

import numpy as np
import jax
import jax.numpy as jnp
from jax import lax
from jax.experimental import pallas as pl
from jax.experimental.pallas import tpu as pltpu

F32 = jnp.float32
BF16 = jnp.bfloat16

D_MODEL = 1024
HEAD_DIM = 64
N_HEADS = 8
WIDTH = N_HEADS * HEAD_DIM
N_SEG = 8
N_PAIR = N_HEADS // 2
LANES = 128
N_META = 16
CHUNK = 128
ROPE_BASE = 10000.0
EPS = 1e-6
QK_SCALE = HEAD_DIM ** -0.5
LOG2E = 1.4426950408889634
EXP2_MAX = 126.0
VMEM_LIMIT = 56 * 1024 * 1024


def _nt_dot(a, b):
    return lax.dot_general(a, b, (((1,), (1,)), ((), ())), preferred_element_type=F32)


def _tn_dot(a, b):
    return lax.dot_general(a, b, (((0,), (0,)), ((), ())), preferred_element_type=F32)


def _split_bf16(x):
    hi = x.astype(BF16)
    lo = (x - hi.astype(F32)).astype(BF16)
    return jnp.concatenate([hi, lo], axis=1)


def _pair_head_mean(sq, g2):
    return jnp.dot(_split_bf16(sq), g2, preferred_element_type=F32)


def _log2_sigmoids(z2):
    zc = jnp.minimum(z2, EXP2_MAX)
    lk2 = jnp.log(1.0 + jnp.exp2(zc)) * (-LOG2E)
    return lk2 + zc, lk2


def _silu(g):
    return g * (1.0 / (1.0 + jnp.exp(-g)))


def _rope_pair(z, cos, sin_signed):
    lane = lax.broadcasted_iota(jnp.int32, z.shape, 1)
    first_half = (lane % HEAD_DIM) < (HEAD_DIM // 2)
    swapped = jnp.where(first_half,
                        pltpu.roll(z, LANES - HEAD_DIM // 2, axis=1),
                        pltpu.roll(z, HEAD_DIM // 2, axis=1))
    return z * cos + swapped * sin_signed


def _proj_kernel(x_ref, nw_ref, w_ref, cos_ref, sin_ref, qnw_ref, knw_ref, g2_ref,
                 qr_ref, kr_ref, vr_ref, gr_ref, qs_ref, ks_ref, vs_ref, gs_ref,
                 *attn_refs):
    x = x_ref[0]
    ms = jnp.mean(x * x, axis=-1, keepdims=True)
    xn = ((x * lax.rsqrt(ms + EPS)) * nw_ref[...]).astype(BF16)
    cos = cos_ref[...]
    sin = sin_ref[...]
    g2 = g2_ref[...]

    def seg(i):
        return jnp.dot(xn, w_ref[:, i * WIDTH:(i + 1) * WIDTH], preferred_element_type=F32)

    def pair(z, p):
        return z[:, p * LANES:(p + 1) * LANES]

    z = seg(0)
    for p in range(N_PAIR):
        qr_ref[0, :, p * LANES:(p + 1) * LANES] = _rope_pair(pair(z, p), cos, sin).astype(BF16)
    z = seg(1)
    for p in range(N_PAIR):
        kr_ref[0, :, p * LANES:(p + 1) * LANES] = (
            _rope_pair(pair(z, p), cos, sin) * QK_SCALE).astype(BF16)
    vr_ref[0] = seg(2).astype(BF16)
    gr_ref[0] = seg(3)
    z = seg(4)
    for p in range(N_PAIR):
        zp = pair(z, p)
        r = lax.rsqrt(_pair_head_mean(zp * zp, g2) + EPS)
        qs_ref[0, :, p * LANES:(p + 1) * LANES] = (
            (zp * r) * qnw_ref[...] * (QK_SCALE * LOG2E)).astype(BF16)
    z = seg(5)
    for p in range(N_PAIR):
        zp = pair(z, p)
        r = lax.rsqrt(_pair_head_mean(zp * zp, g2) + EPS)
        kn = (zp * r) * knw_ref[...]
        ks_ref[0, :, p * LANES:(p + 1) * LANES] = kn
        if attn_refs:
            kt = kn.T.astype(BF16)
            row = lax.broadcasted_iota(jnp.int32, (LANES, CHUNK), 0)
            zero = jnp.zeros((LANES, CHUNK), BF16)
            for u in range(kt.shape[1] // CHUNK):
                ku = kt[:, u * CHUNK:(u + 1) * CHUNK]
                attn_refs[0][0, p, u, :, :CHUNK] = jnp.where(row < HEAD_DIM, ku, zero)
                attn_refs[0][0, p, u, :, CHUNK:] = jnp.where(row < HEAD_DIM, zero, ku)
    z = seg(6)
    vs_ref[0] = z
    if attn_refs:
        lane = lax.broadcasted_iota(jnp.int32, (CHUNK, LANES), 1)
        zero = jnp.zeros((CHUNK, LANES), BF16)
        for p in range(N_PAIR):
            vb = pair(z, p).astype(BF16)
            for u in range(vb.shape[0] // CHUNK):
                vu = vb[u * CHUNK:(u + 1) * CHUNK, :]
                attn_refs[1][0, p, u, :CHUNK, :] = jnp.where(lane < HEAD_DIM, vu, zero)
                attn_refs[1][0, p, u, CHUNK:, :] = jnp.where(lane < HEAD_DIM, zero, vu)
    gs_ref[0] = seg(7)


def _project(x, norm_w, w_in_bf, cos_t, sin_t, qnw2, knw2, g2, tm, attn_blocks=False):
    b, t, _ = x.shape
    grid = (b, t // tm)
    row = lambda i, j: (i, j, 0)
    const2 = lambda i, j: (0, 0)
    out_block = pl.BlockSpec((1, tm, WIDTH), row)
    f32_out = jax.ShapeDtypeStruct((b, t, WIDTH), F32)
    bf_out = jax.ShapeDtypeStruct((b, t, WIDTH), BF16)
    out_specs = [out_block] * 8
    out_shape = [bf_out, bf_out, bf_out, f32_out, bf_out, f32_out, f32_out, f32_out]
    if attn_blocks:
        nb = tm // CHUNK
        blk5 = lambda i, j: (i, 0, j, 0, 0)
        out_specs += [pl.BlockSpec((1, N_PAIR, nb, LANES, 2 * CHUNK), blk5),
                      pl.BlockSpec((1, N_PAIR, nb, 2 * CHUNK, LANES), blk5)]
        out_shape += [jax.ShapeDtypeStruct((b, N_PAIR, t // CHUNK, LANES, 2 * CHUNK), BF16),
                      jax.ShapeDtypeStruct((b, N_PAIR, t // CHUNK, 2 * CHUNK, LANES), BF16)]
    return pl.pallas_call(
        _proj_kernel,
        grid=grid,
        in_specs=[
            pl.BlockSpec((1, tm, D_MODEL), row),
            pl.BlockSpec((1, D_MODEL), const2),
            pl.BlockSpec((D_MODEL, N_SEG * WIDTH), const2),
            pl.BlockSpec((tm, LANES), lambda i, j: (j, 0)),
            pl.BlockSpec((tm, LANES), lambda i, j: (j, 0)),
            pl.BlockSpec((1, LANES), const2),
            pl.BlockSpec((1, LANES), const2),
            pl.BlockSpec((2 * LANES, LANES), const2),
        ],
        out_specs=out_specs,
        out_shape=out_shape,
        compiler_params=pltpu.CompilerParams(
            dimension_semantics=("arbitrary", "arbitrary"),
            vmem_limit_bytes=VMEM_LIMIT),
        name="proj",
    )(x, norm_w, w_in_bf, cos_t, sin_t, qnw2, knw2, g2)


RET_CHUNKS = 2


def _gate_norm_pair(o, g, w, g2):
    r = lax.rsqrt(_pair_head_mean(o * o, g2) + EPS)
    return ((o * r) * w) * _silu(g)


def _ret_kernel(q_ref, k_ref, v_ref, g_ref, km_ref, vm_ref, dmat_ref, cdec_ref, kdec_ref,
                kdecm_ref, gam_ref, bd_ref, w_ref, g2_ref, y_ref, sout_ref, state_ref):
    c = pl.program_id(1)
    bd = bd_ref[...] > 0.0
    g2 = g2_ref[...]
    lane = lax.broadcasted_iota(jnp.int32, (CHUNK, LANES), 1)
    low = lane < HEAD_DIM

    @pl.when(c == 0)
    def _():
        for p in range(N_PAIR):
            sl = slice(p * LANES, (p + 1) * LANES)
            kd = (km_ref[:, sl].astype(F32) * kdecm_ref[:, sl]).astype(BF16)
            state_ref[p] = jnp.where(bd, _tn_dot(kd, vm_ref[:, sl]), 0.0)

    zero = jnp.zeros((CHUNK, LANES), BF16)
    for u in range(RET_CHUNKS):
        rows = slice(u * CHUNK, (u + 1) * CHUNK)
        vs, sc, cross, upd, old = [], [], [], [], []
        for p in range(N_PAIR):
            sl = slice(p * LANES, (p + 1) * LANES)
            q = q_ref[0, rows, sl]
            k = k_ref[0, rows, sl]
            v = v_ref[0, rows, sl]
            s = state_ref[p]
            sc.append([_nt_dot(jnp.where(low, q, zero), k), _nt_dot(jnp.where(low, zero, q), k)])
            cross.append(jnp.dot(q, s.astype(BF16), preferred_element_type=F32))
            kd = (k.astype(F32) * kdec_ref[:, sl]).astype(BF16)
            upd.append(_tn_dot(kd, v))
            vs.append(v)
            old.append(s)
        for p in range(N_PAIR):
            state_ref[p] = gam_ref[p] * old[p] + jnp.where(bd, upd[p], 0.0)
            sc[p] = [(sc[p][e] * dmat_ref[2 * p + e]).astype(BF16) for e in range(2)]
        inner = [[jnp.dot(sc[p][e], vs[p], preferred_element_type=F32) for e in range(2)]
                 for p in range(N_PAIR)]
        for p in range(N_PAIR):
            sl = slice(p * LANES, (p + 1) * LANES)
            o = jnp.where(low, inner[p][0], inner[p][1]) + cross[p] * cdec_ref[:, sl]
            y_ref[0, rows, sl] = _gate_norm_pair(o, g_ref[0, rows, sl], w_ref[:, sl],
                                                 g2).astype(BF16)

    @pl.when(c == pl.num_programs(1) - 1)
    def _():
        sout_ref[0] = state_ref[...]


def _retention_prompt(qr, kr, vr, gr, kr_meta, vr_meta, tabs, w_onorm, g2):
    b, t, _ = qr.shape
    row = lambda i, j: (i, j, 0)
    c2 = lambda i, j: (0, 0)
    c3 = lambda i, j: (0, 0, 0)
    blk = pl.BlockSpec((1, RET_CHUNKS * CHUNK, WIDTH), row)
    return pl.pallas_call(
        _ret_kernel,
        grid=(b, t // (RET_CHUNKS * CHUNK)),
        in_specs=[
            blk, blk, blk, blk,
            pl.BlockSpec((N_META, WIDTH), c2),
            pl.BlockSpec((N_META, WIDTH), c2),
            pl.BlockSpec((N_HEADS, CHUNK, CHUNK), c3),
            pl.BlockSpec((CHUNK, WIDTH), c2),
            pl.BlockSpec((CHUNK, WIDTH), c2),
            pl.BlockSpec((N_META, WIDTH), c2),
            pl.BlockSpec((N_PAIR, LANES, LANES), c3),
            pl.BlockSpec((LANES, LANES), c2),
            pl.BlockSpec((1, WIDTH), c2),
            pl.BlockSpec((2 * LANES, LANES), c2),
        ],
        out_specs=[blk, pl.BlockSpec((1, N_PAIR, LANES, LANES), lambda i, j: (i, 0, 0, 0))],
        out_shape=[jax.ShapeDtypeStruct((b, t, WIDTH), BF16),
                   jax.ShapeDtypeStruct((b, N_PAIR, LANES, LANES), F32)],
        scratch_shapes=[pltpu.VMEM((N_PAIR, LANES, LANES), F32)],
        compiler_params=pltpu.CompilerParams(
            dimension_semantics=("arbitrary", "arbitrary"),
            vmem_limit_bytes=VMEM_LIMIT),
        name="ret_prompt",
    )(qr, kr, vr, gr, kr_meta, vr_meta, tabs["dmat"], tabs["cdec"], tabs["kdec"],
      tabs["kdec_meta"], tabs["gam"], tabs["bd"], w_onorm, g2)


SB_Q = 2 * CHUNK
SB_KB = 2
SB_SPLIT = 4


def _sb_step(q_ref, blocks, bias_ref, tt, acc_ref, carry_ref):
    def mask(vis_of, width):
        key = lax.broadcasted_iota(jnp.int32, (SB_Q, width), 1) % LANES
        return vis_of(lax.broadcasted_iota(jnp.int32, (SB_Q, width), 0), key)

    ls_all, lkb_all, tot_all = [], [], []
    for kt_of, _, vis_of in blocks:
        for p in range(N_PAIR):
            z = jnp.dot(q_ref[0, :, p * LANES:(p + 1) * LANES], kt_of(p),
                        preferred_element_type=F32) + bias_ref[p]
            ls, lk = _log2_sigmoids(z)
            if vis_of is not None:
                lk = jnp.where(mask(vis_of, 2 * LANES), lk, 0.0)
            ls_all.append(ls)
            lkb_all.append(lk.astype(BF16))
            tot_all.append([jnp.sum(lk[:, e * LANES:(e + 1) * LANES], axis=1, keepdims=True)
                            for e in range(2)])
    cc_all = [jnp.dot(lkb, tt, preferred_element_type=F32) for lkb in lkb_all]
    for p in range(N_PAIR):
        a_run = []
        for n, (_, _, vis_of) in enumerate(blocks):
            s = ls_all[n * N_PAIR + p] + cc_all[n * N_PAIR + p]
            for e in range(2):
                h = 2 * p + e
                a = jnp.exp2(s[:, e * LANES:(e + 1) * LANES] + carry_ref[h])
                if vis_of is not None:
                    a = jnp.where(mask(vis_of, LANES), a, 0.0)
                a_run.append(a.astype(BF16))
                carry_ref[h] += tot_all[n * N_PAIR + p][e]
        acc_ref[p] += jnp.dot(jnp.concatenate(a_run, axis=1),
                              jnp.concatenate([v_of(p) for _, v_of, _ in blocks], axis=0),
                              preferred_element_type=F32)


def _sb_prompt_part(k, q_ref, kt_ref, v_ref, ktm_ref, vm_ref, bias_ref, g_ref, w_ref, tt, g2,
                    y_ref, acc_ref, carry_ref):
    i = pl.program_id(1)

    def cached(blk, vis_of):
        return (lambda p: kt_ref[0, p, blk], lambda p: v_ref[0, p, blk], vis_of)

    @pl.when(k == 0)
    def _():
        acc_ref[...] = jnp.zeros_like(acc_ref)
        carry_ref[...] = jnp.zeros_like(carry_ref)
        _sb_step(q_ref,
                 [cached(2 * i + d, lambda r, s, d=d: s < r - d * CHUNK) for d in (1, 0)],
                 bias_ref, tt, acc_ref, carry_ref)

    def body(jj, carry):
        newest = 2 * i - 1 - SB_KB * jj
        _sb_step(q_ref, [cached(newest - n, None) for n in range(SB_KB)],
                 bias_ref, tt, acc_ref, carry_ref)
        return carry

    lax.fori_loop((k * i) // SB_SPLIT, ((k + 1) * i) // SB_SPLIT, body, 0)

    @pl.when(k == SB_SPLIT - 1)
    def _():
        _sb_step(q_ref, [(lambda p: ktm_ref[p], lambda p: vm_ref[p], lambda r, s: s < N_META)],
                 bias_ref, tt, acc_ref, carry_ref)
        for p in range(N_PAIR):
            sl = slice(p * LANES, (p + 1) * LANES)
            y_ref[0, :, sl] = _gate_norm_pair(acc_ref[p], g_ref[0, :, sl], w_ref[:, sl],
                                              g2).astype(BF16)


def _out_kernel(x_ref, yr_ref, ys_ref, w_ref, o_ref):
    o_ref[0] = (x_ref[0]
                + jnp.dot(yr_ref[0], w_ref[:WIDTH, :], preferred_element_type=F32)
                + jnp.dot(ys_ref[0], w_ref[WIDTH:, :], preferred_element_type=F32))


def _out_proj(x, yr, ys, w_out_bf, tm):
    b, t, _ = x.shape
    row = lambda i, j: (i, j, 0)
    return pl.pallas_call(
        _out_kernel,
        grid=(b, t // tm),
        in_specs=[
            pl.BlockSpec((1, tm, D_MODEL), row),
            pl.BlockSpec((1, tm, WIDTH), row),
            pl.BlockSpec((1, tm, WIDTH), row),
            pl.BlockSpec((2 * WIDTH, D_MODEL), lambda i, j: (0, 0)),
        ],
        out_specs=pl.BlockSpec((1, tm, D_MODEL), row),
        out_shape=jax.ShapeDtypeStruct(x.shape, F32),
        compiler_params=pltpu.CompilerParams(
            dimension_semantics=("arbitrary", "arbitrary"),
            vmem_limit_bytes=VMEM_LIMIT),
        name="out_proj",
    )(x, yr, ys, w_out_bf)


def _ret_sample_kernel(q_ref, k_ref, v_ref, g_ref, s_ref, dmat_ref, cdec_ref, kdec_ref,
                       gam_ref, w_ref, y_ref, sout_ref):
    n = q_ref.shape[0] * N_HEADS
    t = q_ref.shape[2]
    q = q_ref[...].reshape(n, t, HEAD_DIM)
    k = k_ref[...].reshape(n, t, HEAD_DIM)
    v = v_ref[...].reshape(n, t, HEAD_DIM)
    s = s_ref[...].reshape(n, HEAD_DIM, HEAD_DIM)
    sc = jnp.einsum("ntd,nsd->nts", q, k, preferred_element_type=F32) * dmat_ref[...]
    inner = jnp.einsum("nts,nse->nte", sc, v, preferred_element_type=F32)
    cross = jnp.einsum("ntd,nde->nte", q, s, preferred_element_type=F32) * cdec_ref[...]
    o = inner + cross
    upd = jnp.einsum("ntd,nte->nde", k * kdec_ref[...], v, preferred_element_type=F32)
    sout_ref[...] = (gam_ref[...] * s + upd).reshape(sout_ref.shape)
    r = lax.rsqrt(jnp.mean(o * o, axis=-1, keepdims=True) + EPS)
    g = g_ref[...].reshape(n, t, HEAD_DIM)
    y_ref[...] = (((o * r) * w_ref[...]) * _silu(g)).reshape(y_ref.shape)


def _retention_sample(q4, k4, v4, g4, state, tabs, w_heads, sblk):
    db, _, t, _ = q4.shape
    blk = pl.BlockSpec((sblk, N_HEADS, t, HEAD_DIM), lambda i: (i, 0, 0, 0))
    sblock = pl.BlockSpec((sblk, N_HEADS, HEAD_DIM, HEAD_DIM), lambda i: (i, 0, 0, 0))
    c3 = lambda i: (0, 0, 0)
    n = sblk * N_HEADS
    return pl.pallas_call(
        _ret_sample_kernel,
        grid=(db // sblk,),
        in_specs=[blk, blk, blk, blk, sblock,
                  pl.BlockSpec((n, t, t), c3),
                  pl.BlockSpec((n, t, 1), c3),
                  pl.BlockSpec((n, t, 1), c3),
                  pl.BlockSpec((n, 1, 1), c3),
                  pl.BlockSpec((n, 1, HEAD_DIM), c3)],
        out_specs=[blk, sblock],
        out_shape=[jax.ShapeDtypeStruct(q4.shape, F32),
                   jax.ShapeDtypeStruct(state.shape, F32)],
        compiler_params=pltpu.CompilerParams(
            dimension_semantics=("arbitrary",),
            vmem_limit_bytes=VMEM_LIMIT),
        name="ret_sample",
    )(q4, k4, v4, g4, state, tabs["dmat"], tabs["cdec"], tabs["kdec"], tabs["gam"], w_heads)


SAMPLE_RUNS = 1


def _sample_pages(qblk, bias, groups, tt, carry, vis):
    rows = qblk.shape[0]
    zs = [jnp.dot(qblk, jnp.concatenate(kts, axis=1), preferred_element_type=F32)
          for kts, _ in groups]
    ls_all, cc_all = [], []
    for (kts, _), z in zip(groups, zs):
        n = len(kts)
        ls, lk = _log2_sigmoids(z + jnp.concatenate([bias] * n, axis=1))
        if vis is not None:
            lk = jnp.where(vis, lk, 0.0)
        lk_rows = jnp.concatenate([lk[:, g * LANES:(g + 1) * LANES] for g in range(n)], axis=0)
        ls_all.append(ls)
        cc_all.append(jnp.dot(_split_bf16(lk_rows), tt, preferred_element_type=F32))
    part = None
    for (kts, vts), ls, cc in zip(groups, ls_all, cc_all):
        a_all = []
        for g in range(len(kts)):
            ccg = cc[g * rows:(g + 1) * rows]
            a = jnp.exp2(ls[:, g * LANES:(g + 1) * LANES] + ccg[:, :LANES] + carry)
            if vis is not None:
                a = jnp.where(vis, a, 0.0)
            a_all.append(a.astype(BF16))
            carry = carry + ccg[:, LANES:]
        a_cat = jnp.concatenate(a_all, axis=1)
        a_pad = jnp.concatenate([a_cat, jnp.zeros((LANES - rows, a_cat.shape[1]), BF16)], axis=0)
        av = _nt_dot(jnp.concatenate(vts, axis=1), a_pad)
        part = av if part is None else part + av
    return part, carry


def _sb_sample_part(k, qblk_ref, bias_ref, ktn_ref, vtn_ref, g_ref, w_ref, tt, g2, k_refs, v_refs,
                    y_ref, acc_ref, carry_ref):
    qblk = qblk_ref[0]
    bias = bias_ref[...]
    t = y_ref.shape[1]
    rows = qblk.shape[0]

    @pl.when(k == 0)
    def _():
        rr = lax.broadcasted_iota(jnp.int32, (rows, LANES), 0)
        col = lax.broadcasted_iota(jnp.int32, (rows, LANES), 1)
        part, carry = _sample_pages(qblk, bias, [([ktn_ref[0]], [vtn_ref[0]])], tt,
                                    jnp.zeros((rows, LANES), F32), col < (rr % t))
        acc_ref[...] = part
        carry_ref[...] = carry

    run = len(k_refs) // SAMPLE_RUNS
    groups = [([r[0].astype(BF16) for r in k_refs[u * run:(u + 1) * run]],
               [r[0].astype(BF16) for r in v_refs[u * run:(u + 1) * run]])
              for u in range(SAMPLE_RUNS)]
    part, carry = _sample_pages(qblk, bias, groups, tt, carry_ref[...], None)
    acc_ref[...] += part
    carry_ref[...] = carry

    @pl.when(k == SB_SPLIT - 1)
    def _():
        acc = acc_ref[...].T
        lane = lax.broadcasted_iota(jnp.int32, (t, WIDTH), 1)
        o = jnp.zeros((t, WIDTH), F32)
        for h in range(N_HEADS):
            o = o + jnp.where(lane // HEAD_DIM == h, acc[h * t:(h + 1) * t, :], 0.0)
        for p in range(N_PAIR):
            sl = slice(p * LANES, (p + 1) * LANES)
            y_ref[0, :, sl] = _gate_norm_pair(o[:, sl], g_ref[0, :, sl], w_ref[:, sl],
                                              g2).astype(BF16)


N_PROMPT_IN = 8
N_SAMPLE_IN = 6


def _sb_fused_kernel(pt_ref, w_ref, g2_ref, *rest, n_group):
    p_in = rest[:N_PROMPT_IN]
    s_in = rest[N_PROMPT_IN:N_PROMPT_IN + N_SAMPLE_IN]
    pages = rest[N_PROMPT_IN + N_SAMPLE_IN:N_PROMPT_IN + N_SAMPLE_IN + 2 * n_group]
    yp_ref, ys_ref, acc_ref, carry_ref, sacc_ref, scarry_ref = rest[-6:]
    q_ref, kt_ref, v_ref, ktm_ref, vm_ref, bias_ref, g_ref, ttp_ref = p_in
    qblk_ref, sbias_ref, ktn_ref, vtn_ref, sg_ref, tts_ref = s_in
    k = pl.program_id(2)
    g2 = g2_ref[...]
    _sb_sample_part(k, qblk_ref, sbias_ref, ktn_ref, vtn_ref, sg_ref, w_ref, tts_ref[...], g2,
                    pages[:n_group], pages[n_group:], ys_ref, sacc_ref, scarry_ref)
    _sb_prompt_part(k, q_ref, kt_ref, v_ref, ktm_ref, vm_ref, bias_ref, g_ref, w_ref,
                    ttp_ref[...], g2, yp_ref, acc_ref, carry_ref)


def _sb_fused(page_table, w_onorm, g2, qs, kt_bd, v_bd, ktm_bd, vm_bd, bias_pairs, gs, ttp,
              qblk, bias_rows, kt_new, vt_new, gs_s, tts, cache_kt, cache_vt):
    b, t, _ = qs.shape
    nblk = t // CHUNK
    n_q = t // SB_Q
    db, n_pages = page_table.shape
    ts = gs_s.shape[1]
    rows = qblk.shape[1]
    page = cache_kt.shape[2]
    n_group = n_pages // SB_SPLIT
    assert db == b * n_q and n_pages % SB_SPLIT == 0

    c2 = lambda i, j, k, pt: (0, 0)
    c3 = lambda i, j, k, pt: (0, 0, 0)
    qrow = lambda i, j, k, pt: (i, j, 0)
    batch5 = lambda i, j, k, pt: (i, 0, 0, 0, 0)
    seq3 = lambda i, j, k, pt: (i * n_q + j, 0, 0)

    def page_map(gidx):
        return lambda i, j, k, pt: (pt[i * n_q + j, n_pages - 1 - (k * n_group + gidx)], 0, 0)

    qblock = pl.BlockSpec((1, SB_Q, WIDTH), qrow)
    once = pl.Buffered(1)
    page_specs = [pl.BlockSpec((1, WIDTH, page), page_map(gidx)) for gidx in range(n_group)]
    grid_spec = pltpu.PrefetchScalarGridSpec(
        num_scalar_prefetch=1,
        grid=(b, n_q, SB_SPLIT),
        in_specs=[
            pl.BlockSpec((1, WIDTH), c2),
            pl.BlockSpec((2 * LANES, LANES), c2),
            qblock,
            pl.BlockSpec((1, N_PAIR, nblk, LANES, 2 * CHUNK), batch5, pipeline_mode=once),
            pl.BlockSpec((1, N_PAIR, nblk, 2 * CHUNK, LANES), batch5, pipeline_mode=once),
            pl.BlockSpec((N_PAIR, LANES, 2 * CHUNK), c3),
            pl.BlockSpec((N_PAIR, 2 * CHUNK, LANES), c3),
            pl.BlockSpec((N_PAIR, 1, 2 * LANES), c3),
            qblock,
            pl.BlockSpec((2 * LANES, 2 * LANES), c2),
            pl.BlockSpec((1, rows, WIDTH), seq3),
            pl.BlockSpec((rows, LANES), c2),
            pl.BlockSpec((1, WIDTH, LANES), seq3),
            pl.BlockSpec((1, WIDTH, LANES), seq3),
            pl.BlockSpec((1, ts, WIDTH), seq3),
            pl.BlockSpec((2 * LANES, 2 * LANES), c2),
        ] + page_specs + page_specs,
        out_specs=[qblock, pl.BlockSpec((1, ts, WIDTH), seq3)],
        scratch_shapes=[pltpu.VMEM((N_PAIR, SB_Q, LANES), F32),
                        pltpu.VMEM((N_HEADS, SB_Q, LANES), F32),
                        pltpu.VMEM((WIDTH, LANES), F32),
                        pltpu.VMEM((rows, LANES), F32)],
    )
    return pl.pallas_call(
        functools.partial(_sb_fused_kernel, n_group=n_group),
        grid_spec=grid_spec,
        out_shape=[jax.ShapeDtypeStruct((b, t, WIDTH), BF16),
                   jax.ShapeDtypeStruct((db, ts, WIDTH), BF16)],
        compiler_params=pltpu.CompilerParams(
            dimension_semantics=("arbitrary", "arbitrary", "arbitrary"),
            vmem_limit_bytes=VMEM_LIMIT),
        name="sb_fused",
    )(page_table, w_onorm, g2, qs, kt_bd, v_bd, ktm_bd, vm_bd, bias_pairs, gs, ttp,
      qblk, bias_rows, kt_new, vt_new, gs_s, tts,
      *([cache_kt] * n_group), *([cache_vt] * n_group))


def _rope_tables(pos):
    half = HEAD_DIM // 2
    inv = ROPE_BASE ** (-jnp.arange(half, dtype=F32) / half)
    ang = pos.astype(F32)[:, None] * inv[None, :]
    cos = jnp.tile(jnp.cos(ang), (1, LANES // half))
    sin = jnp.sin(ang)
    sin_signed = jnp.tile(jnp.concatenate([-sin, sin], axis=1), (1, LANES // HEAD_DIM))
    return cos, sin_signed


def _log_gamma():
    return jnp.log1p(-jnp.exp2(-5.0 - jnp.arange(N_HEADS, dtype=F32)))


def _head_lanes(x):
    return jnp.repeat(x, HEAD_DIM, axis=-1)


def _retention_tables_prompt():
    lg = _log_gamma()
    idx = jnp.arange(CHUNK, dtype=F32)
    rel = idx[:, None] - idx[None, :]
    dmat = jnp.where(rel[None] >= 0, jnp.exp(jnp.maximum(rel, 0.0)[None] * lg[:, None, None]), 0.0)
    cdec = _head_lanes(jnp.exp((idx + 1.0)[:, None] * lg[None, :]))
    kdec = _head_lanes(jnp.exp((CHUNK - 1.0 - idx)[:, None] * lg[None, :]))
    midx = jnp.arange(N_META, dtype=F32)
    kdec_meta = _head_lanes(jnp.exp((N_META - 1.0 - midx)[:, None] * lg[None, :]))
    head_of = jnp.arange(LANES) // HEAD_DIM
    bd = (head_of[:, None] == head_of[None, :]).astype(F32)
    gfull = jnp.exp(CHUNK * lg).reshape(N_PAIR, 2)
    gam = bd[None] * gfull[:, head_of][:, :, None]
    return {"dmat": dmat, "cdec": cdec, "kdec": kdec, "kdec_meta": kdec_meta, "gam": gam, "bd": bd}


def _retention_tables_sample(t, sblk):
    lg = _log_gamma()
    idx = jnp.arange(t, dtype=F32)
    rel = idx[:, None] - idx[None, :]
    dmat = jnp.where(rel[None] >= 0, jnp.exp(jnp.maximum(rel, 0.0)[None] * lg[:, None, None]), 0.0)
    cdec = jnp.exp((idx + 1.0)[None, :] * lg[:, None])[:, :, None]
    kdec = jnp.exp((t - 1.0 - idx)[None, :] * lg[:, None])[:, :, None]
    gam = jnp.exp(t * lg)[:, None, None]
    rep = lambda a: jnp.tile(a, (sblk, 1, 1))
    return {"dmat": rep(dmat), "cdec": rep(cdec), "kdec": rep(kdec), "gam": rep(gam)}


def _pair_mean_matrix():
    head_of = np.arange(LANES) // HEAD_DIM
    g = (head_of[:, None] == head_of[None, :]).astype(np.float32) / HEAD_DIM
    return jnp.asarray(np.concatenate([g, g], axis=0), dtype=BF16)


def _pair_suffix_sum_matrix():
    i = np.arange(LANES)
    tri = (i[:, None] > i[None, :]).astype(np.float32)
    zero = np.zeros((LANES, LANES), np.float32)
    return jnp.asarray(np.block([[tri, zero], [zero, tri]]), dtype=BF16)


def _suffix_sum_matrix_lanes():
    i = np.arange(LANES)
    tri = (i[:, None] > i[None, :]).astype(np.float32)
    half = np.concatenate([tri, np.ones((LANES, LANES), np.float32)], axis=1)
    return jnp.asarray(np.concatenate([half, half], axis=0), dtype=BF16)


def kernel(x_prompt, x_sample, cache_k, cache_v, state_ret, page_table, meta_tokens, norm_w, w_in,
           q_norm_w, k_norm_w, sb_bias, ret_onorm_w, sb_onorm_w, w_out):
    b, seq, _ = x_prompt.shape
    db, t, _ = x_sample.shape
    n_pool, page = cache_k.shape[1], cache_k.shape[2]
    n_pages = page_table.shape[1]
    past_len = n_pages * page

    w_in_bf = w_in[0].astype(BF16)
    w_out_bf = w_out[0].astype(BF16)
    nw = norm_w[0][None, :]
    qnw2 = jnp.tile(q_norm_w[0], LANES // HEAD_DIM)[None, :]
    knw2 = jnp.tile(k_norm_w[0], LANES // HEAD_DIM)[None, :]
    w_ret = ret_onorm_w[0][None, :]
    w_sb = sb_onorm_w[0][None, :]
    bias = sb_bias[0] * LOG2E
    g2 = _pair_mean_matrix()
    tt = _suffix_sum_matrix_lanes()

    cos_m, sin_m = _rope_tables(jnp.arange(N_META))
    cos_p, sin_p = _rope_tables(N_META + jnp.arange(seq))
    _, kr_m, vr_m, _, _, ks_m, vs_m, _ = _project(
        meta_tokens[None], nw, w_in_bf, cos_m, sin_m, qnw2, knw2, g2, N_META)
    qr, kr, vr, gr, qs, ks, vs, gs, kt_bd, v_bd = _project(
        x_prompt, nw, w_in_bf, cos_p, sin_p, qnw2, knw2, g2, 256, attn_blocks=True)

    tabs = _retention_tables_prompt()
    y_r, s_pairs = _retention_prompt(qr, kr, vr, gr, kr_m[0], vr_m[0], tabs, w_ret, g2)

    pad = ((0, CHUNK - N_META), (0, 0))
    km = jnp.pad(ks_m[0], pad).astype(BF16).reshape(CHUNK, N_PAIR, 2, HEAD_DIM)
    vm = jnp.pad(vs_m[0], pad).astype(BF16).reshape(CHUNK, N_PAIR, 2, HEAD_DIM)
    eye2 = jnp.eye(2, dtype=BF16)
    ktm_bd = (km.transpose(1, 2, 3, 0)[:, :, :, None, :] * eye2[None, :, None, :, None]
              ).reshape(N_PAIR, LANES, 2 * CHUNK)
    vm_bd = (vm.transpose(1, 0, 2, 3)[:, None, :, :, :] * eye2[None, :, None, :, None]
             ).reshape(N_PAIR, 2 * CHUNK, LANES)
    bias_pairs = jnp.repeat(bias.reshape(N_PAIR, 1, 2), LANES, axis=2)

    def with_meta(m, a):
        full = jnp.concatenate([jnp.broadcast_to(m, (b, N_META, WIDTH)), a], axis=1)
        return full.reshape(1, b, seq + N_META, N_HEADS, HEAD_DIM)

    new_k_prompt = with_meta(ks_m, ks)
    new_v_prompt = with_meta(vs_m, vs)
    sp = s_pairs.reshape(b, N_PAIR, 2, HEAD_DIM, 2, HEAD_DIM)
    new_state_prompt = jnp.stack([sp[:, :, 0, :, 0, :], sp[:, :, 1, :, 1, :]], axis=2)
    new_state_prompt = new_state_prompt.reshape(1, b, N_HEADS, HEAD_DIM, HEAD_DIM)

    n_tok = db * t
    cos_s, sin_s = _rope_tables(jnp.tile(past_len + jnp.arange(t), db))
    qr_s, kr_s, vr_s, gr_s, qs_s, ks_s, vs_s, gs_s = _project(
        x_sample.reshape(1, n_tok, D_MODEL), nw, w_in_bf, cos_s, sin_s, qnw2, knw2, g2, 256)

    def heads_major(a):
        return a.reshape(db, t, N_HEADS, HEAD_DIM).transpose(0, 2, 1, 3).astype(F32)

    sblk = 8
    tabs_s = _retention_tables_sample(t, sblk)
    w_heads = jnp.tile(ret_onorm_w[0].reshape(N_HEADS, 1, HEAD_DIM), (sblk, 1, 1))
    y_r4, new_state_sample = _retention_sample(
        heads_major(qr_s), heads_major(kr_s), heads_major(vr_s), heads_major(gr_s),
        state_ret[0], tabs_s, w_heads, sblk)
    y_r_s = y_r4.transpose(0, 2, 1, 3).reshape(1, n_tok, WIDTH).astype(BF16)

    q4 = qs_s.reshape(db, t, N_HEADS, HEAD_DIM).transpose(0, 2, 1, 3)
    eye = jnp.eye(N_HEADS, dtype=BF16)
    qblk = (q4[:, :, :, None, :] * eye[None, :, None, :, None]).reshape(db, N_HEADS * t, WIDTH)
    bias_rows = jnp.broadcast_to(jnp.repeat(bias, t)[:, None], (N_HEADS * t, LANES))

    def new_tokens_t(a):
        at = a.reshape(db, t, WIDTH).transpose(0, 2, 1).astype(BF16)
        return jnp.pad(at, ((0, 0), (0, 0), (0, LANES - t)))

    def pages_t(c):
        return c[0].transpose(0, 2, 3, 1).reshape(n_pool, WIDTH, page)

    y_s, y_s_s = _sb_fused(page_table, w_sb, g2,
                           qs, kt_bd, v_bd, ktm_bd, vm_bd, bias_pairs, gs,
                           _pair_suffix_sum_matrix(),
                           qblk, bias_rows, new_tokens_t(ks_s[0]), new_tokens_t(vs_s[0]),
                           gs_s.reshape(db, t, WIDTH), tt, pages_t(cache_k), pages_t(cache_v))
    y_prompt = _out_proj(x_prompt, y_r, y_s, w_out_bf, 512)
    y_sample = _out_proj(x_sample.reshape(1, n_tok, D_MODEL), y_r_s,
                         y_s_s.reshape(1, n_tok, WIDTH), w_out_bf, 512)
    y_sample = y_sample.reshape(db, t, D_MODEL)

    new_k_sample = ks_s.reshape(1, db, t, N_HEADS, HEAD_DIM)
    new_v_sample = vs_s.reshape(1, db, t, N_HEADS, HEAD_DIM)
    return (y_prompt, y_sample, new_k_prompt, new_v_prompt, new_state_prompt,
            new_k_sample, new_v_sample, new_state_sample[None])
```

```python
import functools

import numpy as np
import jax
import jax.numpy as jnp
from jax import lax
from jax.experimental import pallas as pl
from jax.experimental.pallas import tpu as pltpu

F32 = jnp.float32
BF16 = jnp.bfloat16

D_MODEL = 1024
HEAD_DIM = 64
N_HEADS = 8
WIDTH = N_HEADS * HEAD_DIM
N_SEG = 8
N_PAIR = N_HEADS // 2
LANES = 128
N_META = 16
CHUNK = 128
ROPE_BASE = 10000.0
EPS = 1e-6
QK_SCALE = HEAD_DIM ** -0.5
LOG2E = 1.4426950408889634
EXP2_MAX = 126.0
VMEM_LIMIT = 56 * 1024 * 1024


def _nt_dot(a, b):
    return lax.dot_general(a, b, (((1,), (1,)), ((), ())), preferred_element_type=F32)


def _tn_dot(a, b):
    return lax.dot_general(a, b, (((0,), (0,)), ((), ())), preferred_element_type=F32)


def _split_bf16(x):
    hi = x.astype(BF16)
    lo = (x - hi.astype(F32)).astype(BF16)
    return jnp.concatenate([hi, lo], axis=1)


def _pair_head_mean(sq, g2):
    return jnp.dot(_split_bf16(sq), g2, preferred_element_type=F32)


def _log2_sigmoids(z2):
    zc = jnp.minimum(z2, EXP2_MAX)
    lk2 = jnp.log(1.0 + jnp.exp2(zc)) * (-LOG2E)
    return lk2 + zc, lk2


def _silu(g):
    return g * (1.0 / (1.0 + jnp.exp(-g)))


def _rope_pair(z, cos, sin_signed):
    lane = lax.broadcasted_iota(jnp.int32, z.shape, 1)
    first_half = (lane % HEAD_DIM) < (HEAD_DIM // 2)
    swapped = jnp.where(first_half,
                        pltpu.roll(z, LANES - HEAD_DIM // 2, axis=1),
                        pltpu.roll(z, HEAD_DIM // 2, axis=1))
    return z * cos + swapped * sin_signed


def _proj_kernel(x_ref, nw_ref, w_ref, cos_ref, sin_ref, qnw_ref, knw_ref, g2_ref,
                 qr_ref, kr_ref, vr_ref, gr_ref, qs_ref, ks_ref, vs_ref, gs_ref,
                 *attn_refs):
    x = x_ref[0]
    ms = jnp.mean(x * x, axis=-1, keepdims=True)
    xn = ((x * lax.rsqrt(ms + EPS)) * nw_ref[...]).astype(BF16)
    cos = cos_ref[...]
    sin = sin_ref[...]
    g2 = g2_ref[...]

    def seg(i):
        return jnp.dot(xn, w_ref[:, i * WIDTH:(i + 1) * WIDTH], preferred_element_type=F32)

    def pair(z, p):
        return z[:, p * LANES:(p + 1) * LANES]

    z = seg(0)
    for p in range(N_PAIR):
        qr_ref[0, :, p * LANES:(p + 1) * LANES] = _rope_pair(pair(z, p), cos, sin).astype(BF16)
    z = seg(1)
    for p in range(N_PAIR):
        kr_ref[0, :, p * LANES:(p + 1) * LANES] = (
            _rope_pair(pair(z, p), cos, sin) * QK_SCALE).astype(BF16)
    vr_ref[0] = seg(2).astype(BF16)
    gr_ref[0] = seg(3)
    z = seg(4)
    for p in range(N_PAIR):
        zp = pair(z, p)
        r = lax.rsqrt(_pair_head_mean(zp * zp, g2) + EPS)
        qs_ref[0, :, p * LANES:(p + 1) * LANES] = (
            (zp * r) * qnw_ref[...] * (QK_SCALE * LOG2E)).astype(BF16)
    z = seg(5)
    for p in range(N_PAIR):
        zp = pair(z, p)
        r = lax.rsqrt(_pair_head_mean(zp * zp, g2) + EPS)
        kn = (zp * r) * knw_ref[...]
        ks_ref[0, :, p * LANES:(p + 1) * LANES] = kn
        if attn_refs:
            kt = kn.T.astype(BF16)
            row = lax.broadcasted_iota(jnp.int32, (LANES, CHUNK), 0)
            zero = jnp.zeros((LANES, CHUNK), BF16)
            for u in range(kt.shape[1] // CHUNK):
                ku = kt[:, u * CHUNK:(u + 1) * CHUNK]
                attn_refs[0][0, p, u, :, :CHUNK] = jnp.where(row < HEAD_DIM, ku, zero)
                attn_refs[0][0, p, u, :, CHUNK:] = jnp.where(row < HEAD_DIM, zero, ku)
    z = seg(6)
    vs_ref[0] = z
    if attn_refs:
        lane = lax.broadcasted_iota(jnp.int32, (CHUNK, LANES), 1)
        zero = jnp.zeros((CHUNK, LANES), BF16)
        for p in range(N_PAIR):
            vb = pair(z, p).astype(BF16)
            for u in range(vb.shape[0] // CHUNK):
                vu = vb[u * CHUNK:(u + 1) * CHUNK, :]
                attn_refs[1][0, p, u, :CHUNK, :] = jnp.where(lane < HEAD_DIM, vu, zero)
                attn_refs[1][0, p, u, CHUNK:, :] = jnp.where(lane < HEAD_DIM, zero, vu)
    gs_ref[0] = seg(7)


def _project(x, norm_w, w_in_bf, cos_t, sin_t, qnw2, knw2, g2, tm, attn_blocks=False):
    b, t, _ = x.shape
    grid = (b, t // tm)
    row = lambda i, j: (i, j, 0)
    const2 = lambda i, j: (0, 0)
    out_block = pl.BlockSpec((1, tm, WIDTH), row)
    f32_out = jax.ShapeDtypeStruct((b, t, WIDTH), F32)
    bf_out = jax.ShapeDtypeStruct((b, t, WIDTH), BF16)
    out_specs = [out_block] * 8
    out_shape = [bf_out, bf_out, bf_out, f32_out, bf_out, f32_out, f32_out, f32_out]
    if attn_blocks:
        nb = tm // CHUNK
        blk5 = lambda i, j: (i, 0, j, 0, 0)
        out_specs += [pl.BlockSpec((1, N_PAIR, nb, LANES, 2 * CHUNK), blk5),
                      pl.BlockSpec((1, N_PAIR, nb, 2 * CHUNK, LANES), blk5)]
        out_shape += [jax.ShapeDtypeStruct((b, N_PAIR, t // CHUNK, LANES, 2 * CHUNK), BF16),
                      jax.ShapeDtypeStruct((b, N_PAIR, t // CHUNK, 2 * CHUNK, LANES), BF16)]
    return pl.pallas_call(
        _proj_kernel,
        grid=grid,
        in_specs=[
            pl.BlockSpec((1, tm, D_MODEL), row),
            pl.BlockSpec((1, D_MODEL), const2),
            pl.BlockSpec((D_MODEL, N_SEG * WIDTH), const2),
            pl.BlockSpec((tm, LANES), lambda i, j: (j, 0)),
            pl.BlockSpec((tm, LANES), lambda i, j: (j, 0)),
            pl.BlockSpec((1, LANES), const2),
            pl.BlockSpec((1, LANES), const2),
            pl.BlockSpec((2 * LANES, LANES), const2),
        ],
        out_specs=out_specs,
        out_shape=out_shape,
        compiler_params=pltpu.CompilerParams(
            dimension_semantics=("arbitrary", "arbitrary"),
            vmem_limit_bytes=VMEM_LIMIT),
        name="proj",
    )(x, norm_w, w_in_bf, cos_t, sin_t, qnw2, knw2, g2)


RET_CHUNKS = 2


def _gate_norm_pair(o, g, w, g2):
    r = lax.rsqrt(_pair_head_mean(o * o, g2) + EPS)
    return ((o * r) * w) * _silu(g)


def _ret_kernel(q_ref, k_ref, v_ref, g_ref, km_ref, vm_ref, dmat_ref, cdec_ref, kdec_ref,
                kdecm_ref, gam_ref, bd_ref, w_ref, g2_ref, y_ref, sout_ref, state_ref):
    c = pl.program_id(1)
    bd = bd_ref[...] > 0.0
    g2 = g2_ref[...]
    lane = lax.broadcasted_iota(jnp.int32, (CHUNK, LANES), 1)
    low = lane < HEAD_DIM

    @pl.when(c == 0)
    def _():
        for p in range(N_PAIR):
            sl = slice(p * LANES, (p + 1) * LANES)
            kd = (km_ref[:, sl].astype(F32) * kdecm_ref[:, sl]).astype(BF16)
            state_ref[p] = jnp.where(bd, _tn_dot(kd, vm_ref[:, sl]), 0.0)

    zero = jnp.zeros((CHUNK, LANES), BF16)
    for u in range(RET_CHUNKS):
        rows = slice(u * CHUNK, (u + 1) * CHUNK)
        vs, sc, cross, upd, old = [], [], [], [], []
        for p in range(N_PAIR):
            sl = slice(p * LANES, (p + 1) * LANES)
            q = q_ref[0, rows, sl]
            k = k_ref[0, rows, sl]
            v = v_ref[0, rows, sl]
            s = state_ref[p]
            sc.append([_nt_dot(jnp.where(low, q, zero), k), _nt_dot(jnp.where(low, zero, q), k)])
            cross.append(jnp.dot(q, s.astype(BF16), preferred_element_type=F32))
            kd = (k.astype(F32) * kdec_ref[:, sl]).astype(BF16)
            upd.append(_tn_dot(kd, v))
            vs.append(v)
            old.append(s)
        for p in range(N_PAIR):
            state_ref[p] = gam_ref[p] * old[p] + jnp.where(bd, upd[p], 0.0)
            sc[p] = [(sc[p][e] * dmat_ref[2 * p + e]).astype(BF16) for e in range(2)]
        inner = [[jnp.dot(sc[p][e], vs[p], preferred_element_type=F32) for e in range(2)]
                 for p in range(N_PAIR)]
        for p in range(N_PAIR):
            sl = slice(p * LANES, (p + 1) * LANES)
            o = jnp.where(low, inner[p][0], inner[p][1]) + cross[p] * cdec_ref[:, sl]
            y_ref[0, rows, sl] = _gate_norm_pair(o, g_ref[0, rows, sl], w_ref[:, sl],
                                                 g2).astype(BF16)

    @pl.when(c == pl.num_programs(1) - 1)
    def _():
        sout_ref[0] = state_ref[...]


def _retention_prompt(qr, kr, vr, gr, kr_meta, vr_meta, tabs, w_onorm, g2):
    b, t, _ = qr.shape
    row = lambda i, j: (i, j, 0)
    c2 = lambda i, j: (0, 0)
    c3 = lambda i, j: (0, 0, 0)
    blk = pl.BlockSpec((1, RET_CHUNKS * CHUNK, WIDTH), row)
    return pl.pallas_call(
        _ret_kernel,
        grid=(b, t // (RET_CHUNKS * CHUNK)),
        in_specs=[
            blk, blk, blk, blk,
            pl.BlockSpec((N_META, WIDTH), c2),
            pl.BlockSpec((N_META, WIDTH), c2),
            pl.BlockSpec((N_HEADS, CHUNK, CHUNK), c3),
            pl.BlockSpec((CHUNK, WIDTH), c2),
            pl.BlockSpec((CHUNK, WIDTH), c2),
            pl.BlockSpec((N_META, WIDTH), c2),
            pl.BlockSpec((N_PAIR, LANES, LANES), c3),
            pl.BlockSpec((LANES, LANES), c2),
            pl.BlockSpec((1, WIDTH), c2),
            pl.BlockSpec((2 * LANES, LANES), c2),
        ],
        out_specs=[blk, pl.BlockSpec((1, N_PAIR, LANES, LANES), lambda i, j: (i, 0, 0, 0))],
        out_shape=[jax.ShapeDtypeStruct((b, t, WIDTH), BF16),
                   jax.ShapeDtypeStruct((b, N_PAIR, LANES, LANES), F32)],
        scratch_shapes=[pltpu.VMEM((N_PAIR, LANES, LANES), F32)],
        compiler_params=pltpu.CompilerParams(
            dimension_semantics=("arbitrary", "arbitrary"),
            vmem_limit_bytes=VMEM_LIMIT),
        name="ret_prompt",
    )(qr, kr, vr, gr, kr_meta, vr_meta, tabs["dmat"], tabs["cdec"], tabs["kdec"],
      tabs["kdec_meta"], tabs["gam"], tabs["bd"], w_onorm, g2)


SB_Q = 2 * CHUNK
SB_KB = 2
SB_SPLIT = 4


def _sb_step(q_ref, blocks, bias_ref, tt, acc_ref, carry_ref):
    def mask(vis_of, width):
        key = lax.broadcasted_iota(jnp.int32, (SB_Q, width), 1) % LANES
        return vis_of(lax.broadcasted_iota(jnp.int32, (SB_Q, width), 0), key)

    ls_all, lkb_all, tot_all = [], [], []
    for kt_of, _, vis_of in blocks:
        for p in range(N_PAIR):
            z = jnp.dot(q_ref[0, :, p * LANES:(p + 1) * LANES], kt_of(p),
                        preferred_element_type=F32) + bias_ref[p]
            ls, lk = _log2_sigmoids(z)
            if vis_of is not None:
                lk = jnp.where(mask(vis_of, 2 * LANES), lk, 0.0)
            ls_all.append(ls)
            lkb_all.append(lk.astype(BF16))
            tot_all.append([jnp.sum(lk[:, e * LANES:(e + 1) * LANES], axis=1, keepdims=True)
                            for e in range(2)])
    cc_all = [jnp.dot(lkb, tt, preferred_element_type=F32) for lkb in lkb_all]
    for p in range(N_PAIR):
        a_run = []
        for n, (_, _, vis_of) in enumerate(blocks):
            s = ls_all[n * N_PAIR + p] + cc_all[n * N_PAIR + p]
            for e in range(2):
                h = 2 * p + e
                a = jnp.exp2(s[:, e * LANES:(e + 1) * LANES] + carry_ref[h])
                if vis_of is not None:
                    a = jnp.where(mask(vis_of, LANES), a, 0.0)
                a_run.append(a.astype(BF16))
                carry_ref[h] += tot_all[n * N_PAIR + p][e]
        acc_ref[p] += jnp.dot(jnp.concatenate(a_run, axis=1),
                              jnp.concatenate([v_of(p) for _, v_of, _ in blocks], axis=0),
                              preferred_element_type=F32)


def _sb_prompt_part(k, q_ref, kt_ref, v_ref, ktm_ref, vm_ref, bias_ref, g_ref, w_ref, tt, g2,
                    y_ref, acc_ref, carry_ref):
    i = pl.program_id(1)

    def cached(blk, vis_of):
        return (lambda p: kt_ref[0, p, blk], lambda p: v_ref[0, p, blk], vis_of)

    @pl.when(k == 0)
    def _():
        acc_ref[...] = jnp.zeros_like(acc_ref)
        carry_ref[...] = jnp.zeros_like(carry_ref)
        _sb_step(q_ref,
                 [cached(2 * i + d, lambda r, s, d=d: s < r - d * CHUNK) for d in (1, 0)],
                 bias_ref, tt, acc_ref, carry_ref)

    def body(jj, carry):
        newest = 2 * i - 1 - SB_KB * jj
        _sb_step(q_ref, [cached(newest - n, None) for n in range(SB_KB)],
                 bias_ref, tt, acc_ref, carry_ref)
        return carry

    lax.fori_loop((k * i) // SB_SPLIT, ((k + 1) * i) // SB_SPLIT, body, 0)

    @pl.when(k == SB_SPLIT - 1)
    def _():
        _sb_step(q_ref, [(lambda p: ktm_ref[p], lambda p: vm_ref[p], lambda r, s: s < N_META)],
                 bias_ref, tt, acc_ref, carry_ref)
        for p in range(N_PAIR):
            sl = slice(p * LANES, (p + 1) * LANES)
            y_ref[0, :, sl] = _gate_norm_pair(acc_ref[p], g_ref[0, :, sl], w_ref[:, sl],
                                              g2).astype(BF16)


def _out_kernel(x_ref, yr_ref, ys_ref, w_ref, o_ref):
    o_ref[0] = (x_ref[0]
                + jnp.dot(yr_ref[0], w_ref[:WIDTH, :], preferred_element_type=F32)
                + jnp.dot(ys_ref[0], w_ref[WIDTH:, :], preferred_element_type=F32))


def _out_proj(x, yr, ys, w_out_bf, tm):
    b, t, _ = x.shape
    row = lambda i, j: (i, j, 0)
    return pl.pallas_call(
        _out_kernel,
        grid=(b, t // tm),
        in_specs=[
            pl.BlockSpec((1, tm, D_MODEL), row),
            pl.BlockSpec((1, tm, WIDTH), row),
            pl.BlockSpec((1, tm, WIDTH), row),
            pl.BlockSpec((2 * WIDTH, D_MODEL), lambda i, j: (0, 0)),
        ],
        out_specs=pl.BlockSpec((1, tm, D_MODEL), row),
        out_shape=jax.ShapeDtypeStruct(x.shape, F32),
        compiler_params=pltpu.CompilerParams(
            dimension_semantics=("arbitrary", "arbitrary"),
            vmem_limit_bytes=VMEM_LIMIT),
        name="out_proj",
    )(x, yr, ys, w_out_bf)


def _ret_sample_kernel(q_ref, k_ref, v_ref, g_ref, s_ref, dmat_ref, cdec_ref, kdec_ref,
                       gam_ref, w_ref, y_ref, sout_ref):
    n = q_ref.shape[0] * N_HEADS
    t = q_ref.shape[2]
    q = q_ref[...].reshape(n, t, HEAD_DIM)
    k = k_ref[...].reshape(n, t, HEAD_DIM)
    v = v_ref[...].reshape(n, t, HEAD_DIM)
    s = s_ref[...].reshape(n, HEAD_DIM, HEAD_DIM)
    sc = jnp.einsum("ntd,nsd->nts", q, k, preferred_element_type=F32) * dmat_ref[...]
    inner = jnp.einsum("nts,nse->nte", sc, v, preferred_element_type=F32)
    cross = jnp.einsum("ntd,nde->nte", q, s, preferred_element_type=F32) * cdec_ref[...]
    o = inner + cross
    upd = jnp.einsum("ntd,nte->nde", k * kdec_ref[...], v, preferred_element_type=F32)
    sout_ref[...] = (gam_ref[...] * s + upd).reshape(sout_ref.shape)
    r = lax.rsqrt(jnp.mean(o * o, axis=-1, keepdims=True) + EPS)
    g = g_ref[...].reshape(n, t, HEAD_DIM)
    y_ref[...] = (((o * r) * w_ref[...]) * _silu(g)).reshape(y_ref.shape)


def _retention_sample(q4, k4, v4, g4, state, tabs, w_heads, sblk):
    db, _, t, _ = q4.shape
    blk = pl.BlockSpec((sblk, N_HEADS, t, HEAD_DIM), lambda i: (i, 0, 0, 0))
    sblock = pl.BlockSpec((sblk, N_HEADS, HEAD_DIM, HEAD_DIM), lambda i: (i, 0, 0, 0))
    c3 = lambda i: (0, 0, 0)
    n = sblk * N_HEADS
    return pl.pallas_call(
        _ret_sample_kernel,
        grid=(db // sblk,),
        in_specs=[blk, blk, blk, blk, sblock,
                  pl.BlockSpec((n, t, t), c3),
                  pl.BlockSpec((n, t, 1), c3),
                  pl.BlockSpec((n, t, 1), c3),
                  pl.BlockSpec((n, 1, 1), c3),
                  pl.BlockSpec((n, 1, HEAD_DIM), c3)],
        out_specs=[blk, sblock],
        out_shape=[jax.ShapeDtypeStruct(q4.shape, F32),
                   jax.ShapeDtypeStruct(state.shape, F32)],
        compiler_params=pltpu.CompilerParams(
            dimension_semantics=("arbitrary",),
            vmem_limit_bytes=VMEM_LIMIT),
        name="ret_sample",
    )(q4, k4, v4, g4, state, tabs["dmat"], tabs["cdec"], tabs["kdec"], tabs["gam"], w_heads)


SAMPLE_RUNS = 1


def _sample_pages(qblk, bias, groups, tt, carry, vis):
    rows = qblk.shape[0]
    zs = [jnp.dot(qblk, jnp.concatenate(kts, axis=1), preferred_element_type=F32)
          for kts, _ in groups]
    ls_all, cc_all = [], []
    for (kts, _), z in zip(groups, zs):
        n = len(kts)
        ls, lk = _log2_sigmoids(z + jnp.concatenate([bias] * n, axis=1))
        if vis is not None:
            lk = jnp.where(vis, lk, 0.0)
        lk_rows = jnp.concatenate([lk[:, g * LANES:(g + 1) * LANES] for g in range(n)], axis=0)
        ls_all.append(ls)
        cc_all.append(jnp.dot(_split_bf16(lk_rows), tt, preferred_element_type=F32))
    part = None
    for (kts, vts), ls, cc in zip(groups, ls_all, cc_all):
        a_all = []
        for g in range(len(kts)):
            ccg = cc[g * rows:(g + 1) * rows]
            a = jnp.exp2(ls[:, g * LANES:(g + 1) * LANES] + ccg[:, :LANES] + carry)
            if vis is not None:
                a = jnp.where(vis, a, 0.0)
            a_all.append(a.astype(BF16))
            carry = carry + ccg[:, LANES:]
        a_cat = jnp.concatenate(a_all, axis=1)
        a_pad = jnp.concatenate([a_cat, jnp.zeros((LANES - rows, a_cat.shape[1]), BF16)], axis=0)
        av = _nt_dot(jnp.concatenate(vts, axis=1), a_pad)
        part = av if part is None else part + av
    return part, carry


def _sb_sample_part(k, qblk_ref, bias_ref, ktn_ref, vtn_ref, g_ref, w_ref, tt, g2, k_refs, v_refs,
                    y_ref, acc_ref, carry_ref):
    qblk = qblk_ref[0]
    bias = bias_ref[...]
    t = y_ref.shape[1]
    rows = qblk.shape[0]

    @pl.when(k == 0)
    def _():
        rr = lax.broadcasted_iota(jnp.int32, (rows, LANES), 0)
        col = lax.broadcasted_iota(jnp.int32, (rows, LANES), 1)
        part, carry = _sample_pages(qblk, bias, [([ktn_ref[0]], [vtn_ref[0]])], tt,
                                    jnp.zeros((rows, LANES), F32), col < (rr % t))
        acc_ref[...] = part
        carry_ref[...] = carry

    run = len(k_refs) // SAMPLE_RUNS
    groups = [([r[...].astype(BF16) for r in k_refs[u * run:(u + 1) * run]],
               [r[...].astype(BF16) for r in v_refs[u * run:(u + 1) * run]])
              for u in range(SAMPLE_RUNS)]
    part, carry = _sample_pages(qblk, bias, groups, tt, carry_ref[...], None)
    acc_ref[...] += part
    carry_ref[...] = carry

    @pl.when(k == SB_SPLIT - 1)
    def _():
        acc = acc_ref[...].T
        lane = lax.broadcasted_iota(jnp.int32, (t, WIDTH), 1)
        o = jnp.zeros((t, WIDTH), F32)
        for h in range(N_HEADS):
            o = o + jnp.where(lane // HEAD_DIM == h, acc[h * t:(h + 1) * t, :], 0.0)
        for p in range(N_PAIR):
            sl = slice(p * LANES, (p + 1) * LANES)
            y_ref[0, :, sl] = _gate_norm_pair(o[:, sl], g_ref[0, :, sl], w_ref[:, sl],
                                              g2).astype(BF16)


N_PROMPT_IN = 8
N_SAMPLE_IN = 6


def _page_copies(pt_ref, kt_hbm, vt_hbm, kbuf, vbuf, sem, step, n_group):
    n_pages = pt_ref.shape[1]
    slot = step % 2
    seq = step // SB_SPLIT
    newest = n_pages - 1 - (step % SB_SPLIT) * n_group
    copies = []
    for g in range(n_group):
        pid = pt_ref[seq, newest - g]
        copies.append(pltpu.make_async_copy(kt_hbm.at[pid], kbuf.at[slot, g], sem.at[slot, 0]))
        copies.append(pltpu.make_async_copy(vt_hbm.at[pid], vbuf.at[slot, g], sem.at[slot, 1]))
    return copies


def _sb_fused_kernel(pt_ref, w_ref, g2_ref, *rest):
    p_in = rest[:N_PROMPT_IN]
    s_in = rest[N_PROMPT_IN:N_PROMPT_IN + N_SAMPLE_IN]
    kt_hbm, vt_hbm = rest[N_PROMPT_IN + N_SAMPLE_IN:N_PROMPT_IN + N_SAMPLE_IN + 2]
    yp_ref, ys_ref, acc_ref, carry_ref, sacc_ref, scarry_ref, kbuf, vbuf, sem = rest[-9:]
    q_ref, kt_ref, v_ref, ktm_ref, vm_ref, bias_ref, g_ref, ttp_ref = p_in
    qblk_ref, sbias_ref, ktn_ref, vtn_ref, sg_ref, tts_ref = s_in
    n_group = kbuf.shape[1]
    k = pl.program_id(2)
    step = (pl.program_id(0) * pl.num_programs(1) + pl.program_id(1)) * SB_SPLIT + k
    n_steps = pl.num_programs(0) * pl.num_programs(1) * SB_SPLIT

    def copies(s):
        return _page_copies(pt_ref, kt_hbm, vt_hbm, kbuf, vbuf, sem, s, n_group)

    @pl.when(step == 0)
    def _():
        for c in copies(step):
            c.start()

    @pl.when(step + 1 < n_steps)
    def _():
        for c in copies(step + 1):
            c.start()

    for c in copies(step):
        c.wait()

    slot = step % 2
    g2 = g2_ref[...]
    _sb_sample_part(k, qblk_ref, sbias_ref, ktn_ref, vtn_ref, sg_ref, w_ref, tts_ref[...], g2,
                    [kbuf.at[slot, g] for g in range(n_group)],
                    [vbuf.at[slot, g] for g in range(n_group)], ys_ref, sacc_ref, scarry_ref)
    _sb_prompt_part(k, q_ref, kt_ref, v_ref, ktm_ref, vm_ref, bias_ref, g_ref, w_ref,
                    ttp_ref[...], g2, yp_ref, acc_ref, carry_ref)


def _sb_fused(page_table, w_onorm, g2, qs, kt_bd, v_bd, ktm_bd, vm_bd, bias_pairs, gs, ttp,
              qblk, bias_rows, kt_new, vt_new, gs_s, tts, cache_kt, cache_vt):
    b, t, _ = qs.shape
    nblk = t // CHUNK
    n_q = t // SB_Q
    db, n_pages = page_table.shape
    ts = gs_s.shape[1]
    rows = qblk.shape[1]
    page = cache_kt.shape[2]
    n_group = n_pages // SB_SPLIT
    assert db == b * n_q and n_pages % SB_SPLIT == 0

    c2 = lambda i, j, k, pt: (0, 0)
    c3 = lambda i, j, k, pt: (0, 0, 0)
    qrow = lambda i, j, k, pt: (i, j, 0)
    batch5 = lambda i, j, k, pt: (i, 0, 0, 0, 0)
    seq3 = lambda i, j, k, pt: (i * n_q + j, 0, 0)
    qblock = pl.BlockSpec((1, SB_Q, WIDTH), qrow)
    once = pl.Buffered(1)
    grid_spec = pltpu.PrefetchScalarGridSpec(
        num_scalar_prefetch=1,
        grid=(b, n_q, SB_SPLIT),
        in_specs=[
            pl.BlockSpec((1, WIDTH), c2),
            pl.BlockSpec((2 * LANES, LANES), c2),
            qblock,
            pl.BlockSpec((1, N_PAIR, nblk, LANES, 2 * CHUNK), batch5, pipeline_mode=once),
            pl.BlockSpec((1, N_PAIR, nblk, 2 * CHUNK, LANES), batch5, pipeline_mode=once),
            pl.BlockSpec((N_PAIR, LANES, 2 * CHUNK), c3),
            pl.BlockSpec((N_PAIR, 2 * CHUNK, LANES), c3),
            pl.BlockSpec((N_PAIR, 1, 2 * LANES), c3),
            qblock,
            pl.BlockSpec((2 * LANES, 2 * LANES), c2),
            pl.BlockSpec((1, rows, WIDTH), seq3),
            pl.BlockSpec((rows, LANES), c2),
            pl.BlockSpec((1, WIDTH, LANES), seq3),
            pl.BlockSpec((1, WIDTH, LANES), seq3),
            pl.BlockSpec((1, ts, WIDTH), seq3),
            pl.BlockSpec((2 * LANES, 2 * LANES), c2),
            pl.BlockSpec(memory_space=pl.ANY),
            pl.BlockSpec(memory_space=pl.ANY),
        ],
        out_specs=[qblock, pl.BlockSpec((1, ts, WIDTH), seq3)],
        scratch_shapes=[pltpu.VMEM((N_PAIR, SB_Q, LANES), F32),
                        pltpu.VMEM((N_HEADS, SB_Q, LANES), F32),
                        pltpu.VMEM((WIDTH, LANES), F32),
                        pltpu.VMEM((rows, LANES), F32),
                        pltpu.VMEM((2, n_group, WIDTH, page), F32),
                        pltpu.VMEM((2, n_group, WIDTH, page), F32),
                        pltpu.SemaphoreType.DMA((2, 2))],
    )
    return pl.pallas_call(
        _sb_fused_kernel,
        grid_spec=grid_spec,
        out_shape=[jax.ShapeDtypeStruct((b, t, WIDTH), BF16),
                   jax.ShapeDtypeStruct((db, ts, WIDTH), BF16)],
        compiler_params=pltpu.CompilerParams(
            dimension_semantics=("arbitrary", "arbitrary", "arbitrary"),
            vmem_limit_bytes=VMEM_LIMIT),
        name="sb_fused",
    )(page_table, w_onorm, g2, qs, kt_bd, v_bd, ktm_bd, vm_bd, bias_pairs, gs, ttp,
      qblk, bias_rows, kt_new, vt_new, gs_s, tts, cache_kt, cache_vt)


def _rope_tables(pos):
    half = HEAD_DIM // 2
    inv = ROPE_BASE ** (-jnp.arange(half, dtype=F32) / half)
    ang = pos.astype(F32)[:, None] * inv[None, :]
    cos = jnp.tile(jnp.cos(ang), (1, LANES // half))
    sin = jnp.sin(ang)
    sin_signed = jnp.tile(jnp.concatenate([-sin, sin], axis=1), (1, LANES // HEAD_DIM))
    return cos, sin_signed


def _log_gamma():
    return jnp.log1p(-jnp.exp2(-5.0 - jnp.arange(N_HEADS, dtype=F32)))


def _head_lanes(x):
    return jnp.repeat(x, HEAD_DIM, axis=-1)


def _retention_tables_prompt():
    lg = _log_gamma()
    idx = jnp.arange(CHUNK, dtype=F32)
    rel = idx[:, None] - idx[None, :]
    dmat = jnp.where(rel[None] >= 0, jnp.exp(jnp.maximum(rel, 0.0)[None] * lg[:, None, None]), 0.0)
    cdec = _head_lanes(jnp.exp((idx + 1.0)[:, None] * lg[None, :]))
    kdec = _head_lanes(jnp.exp((CHUNK - 1.0 - idx)[:, None] * lg[None, :]))
    midx = jnp.arange(N_META, dtype=F32)
    kdec_meta = _head_lanes(jnp.exp((N_META - 1.0 - midx)[:, None] * lg[None, :]))
    head_of = jnp.arange(LANES) // HEAD_DIM
    bd = (head_of[:, None] == head_of[None, :]).astype(F32)
    gfull = jnp.exp(CHUNK * lg).reshape(N_PAIR, 2)
    gam = bd[None] * gfull[:, head_of][:, :, None]
    return {"dmat": dmat, "cdec": cdec, "kdec": kdec, "kdec_meta": kdec_meta, "gam": gam, "bd": bd}


def _retention_tables_sample(t, sblk):
    lg = _log_gamma()
    idx = jnp.arange(t, dtype=F32)
    rel = idx[:, None] - idx[None, :]
    dmat = jnp.where(rel[None] >= 0, jnp.exp(jnp.maximum(rel, 0.0)[None] * lg[:, None, None]), 0.0)
    cdec = jnp.exp((idx + 1.0)[None, :] * lg[:, None])[:, :, None]
    kdec = jnp.exp((t - 1.0 - idx)[None, :] * lg[:, None])[:, :, None]
    gam = jnp.exp(t * lg)[:, None, None]
    rep = lambda a: jnp.tile(a, (sblk, 1, 1))
    return {"dmat": rep(dmat), "cdec": rep(cdec), "kdec": rep(kdec), "gam": rep(gam)}


def _pair_mean_matrix():
    head_of = np.arange(LANES) // HEAD_DIM
    g = (head_of[:, None] == head_of[None, :]).astype(np.float32) / HEAD_DIM
    return jnp.asarray(np.concatenate([g, g], axis=0), dtype=BF16)


def _pair_suffix_sum_matrix():
    i = np.arange(LANES)
    tri = (i[:, None] > i[None, :]).astype(np.float32)
    zero = np.zeros((LANES, LANES), np.float32)
    return jnp.asarray(np.block([[tri, zero], [zero, tri]]), dtype=BF16)


def _suffix_sum_matrix_lanes():
    i = np.arange(LANES)
    tri = (i[:, None] > i[None, :]).astype(np.float32)
    half = np.concatenate([tri, np.ones((LANES, LANES), np.float32)], axis=1)
    return jnp.asarray(np.concatenate([half, half], axis=0), dtype=BF16)


def kernel(x_prompt, x_sample, cache_k, cache_v, state_ret, page_table, meta_tokens, norm_w, w_in,
           q_norm_w, k_norm_w, sb_bias, ret_onorm_w, sb_onorm_w, w_out):
    b, seq, _ = x_prompt.shape
    db, t, _ = x_sample.shape
    n_pool, page = cache_k.shape[1], cache_k.shape[2]
    n_pages = page_table.shape[1]
    past_len = n_pages * page

    w_in_bf = w_in[0].astype(BF16)
    w_out_bf = w_out[0].astype(BF16)
    nw = norm_w[0][None, :]
    qnw2 = jnp.tile(q_norm_w[0], LANES // HEAD_DIM)[None, :]
    knw2 = jnp.tile(k_norm_w[0], LANES // HEAD_DIM)[None, :]
    w_ret = ret_onorm_w[0][None, :]
    w_sb = sb_onorm_w[0][None, :]
    bias = sb_bias[0] * LOG2E
    g2 = _pair_mean_matrix()
    tt = _suffix_sum_matrix_lanes()

    cos_m, sin_m = _rope_tables(jnp.arange(N_META))
    cos_p, sin_p = _rope_tables(N_META + jnp.arange(seq))
    _, kr_m, vr_m, _, _, ks_m, vs_m, _ = _project(
        meta_tokens[None], nw, w_in_bf, cos_m, sin_m, qnw2, knw2, g2, N_META)
    qr, kr, vr, gr, qs, ks, vs, gs, kt_bd, v_bd = _project(
        x_prompt, nw, w_in_bf, cos_p, sin_p, qnw2, knw2, g2, 256, attn_blocks=True)

    tabs = _retention_tables_prompt()
    y_r, s_pairs = _retention_prompt(qr, kr, vr, gr, kr_m[0], vr_m[0], tabs, w_ret, g2)

    pad = ((0, CHUNK - N_META), (0, 0))
    km = jnp.pad(ks_m[0], pad).astype(BF16).reshape(CHUNK, N_PAIR, 2, HEAD_DIM)
    vm = jnp.pad(vs_m[0], pad).astype(BF16).reshape(CHUNK, N_PAIR, 2, HEAD_DIM)
    eye2 = jnp.eye(2, dtype=BF16)
    ktm_bd = (km.transpose(1, 2, 3, 0)[:, :, :, None, :] * eye2[None, :, None, :, None]
              ).reshape(N_PAIR, LANES, 2 * CHUNK)
    vm_bd = (vm.transpose(1, 0, 2, 3)[:, None, :, :, :] * eye2[None, :, None, :, None]
             ).reshape(N_PAIR, 2 * CHUNK, LANES)
    bias_pairs = jnp.repeat(bias.reshape(N_PAIR, 1, 2), LANES, axis=2)

    def with_meta(m, a):
        full = jnp.concatenate([jnp.broadcast_to(m, (b, N_META, WIDTH)), a], axis=1)
        return full.reshape(1, b, seq + N_META, N_HEADS, HEAD_DIM)

    new_k_prompt = with_meta(ks_m, ks)
    new_v_prompt = with_meta(vs_m, vs)
    sp = s_pairs.reshape(b, N_PAIR, 2, HEAD_DIM, 2, HEAD_DIM)
    new_state_prompt = jnp.stack([sp[:, :, 0, :, 0, :], sp[:, :, 1, :, 1, :]], axis=2)
    new_state_prompt = new_state_prompt.reshape(1, b, N_HEADS, HEAD_DIM, HEAD_DIM)

    n_tok = db * t
    cos_s, sin_s = _rope_tables(jnp.tile(past_len + jnp.arange(t), db))
    qr_s, kr_s, vr_s, gr_s, qs_s, ks_s, vs_s, gs_s = _project(
        x_sample.reshape(1, n_tok, D_MODEL), nw, w_in_bf, cos_s, sin_s, qnw2, knw2, g2, 256)

    def heads_major(a):
        return a.reshape(db, t, N_HEADS, HEAD_DIM).transpose(0, 2, 1, 3).astype(F32)

    sblk = 8
    tabs_s = _retention_tables_sample(t, sblk)
    w_heads = jnp.tile(ret_onorm_w[0].reshape(N_HEADS, 1, HEAD_DIM), (sblk, 1, 1))
    y_r4, new_state_sample = _retention_sample(
        heads_major(qr_s), heads_major(kr_s), heads_major(vr_s), heads_major(gr_s),
        state_ret[0], tabs_s, w_heads, sblk)
    y_r_s = y_r4.transpose(0, 2, 1, 3).reshape(1, n_tok, WIDTH).astype(BF16)

    q4 = qs_s.reshape(db, t, N_HEADS, HEAD_DIM).transpose(0, 2, 1, 3)
    eye = jnp.eye(N_HEADS, dtype=BF16)
    qblk = (q4[:, :, :, None, :] * eye[None, :, None, :, None]).reshape(db, N_HEADS * t, WIDTH)
    bias_rows = jnp.broadcast_to(jnp.repeat(bias, t)[:, None], (N_HEADS * t, LANES))

    def new_tokens_t(a):
        at = a.reshape(db, t, WIDTH).transpose(0, 2, 1).astype(BF16)
        return jnp.pad(at, ((0, 0), (0, 0), (0, LANES - t)))

    def pages_t(c):
        return c[0].transpose(0, 2, 3, 1).reshape(n_pool, WIDTH, page)

    y_s, y_s_s = _sb_fused(page_table, w_sb, g2,
                           qs, kt_bd, v_bd, ktm_bd, vm_bd, bias_pairs, gs,
                           _pair_suffix_sum_matrix(),
                           qblk, bias_rows, new_tokens_t(ks_s[0]), new_tokens_t(vs_s[0]),
                           gs_s.reshape(db, t, WIDTH), tt, pages_t(cache_k), pages_t(cache_v))
    y_prompt = _out_proj(x_prompt, y_r, y_s, w_out_bf, 512)
    y_sample = _out_proj(x_sample.reshape(1, n_tok, D_MODEL), y_r_s,
                         y_s_s.reshape(1, n_tok, WIDTH), w_out_bf, 512)
    y_sample = y_sample.reshape(db, t, D_MODEL)

    new_k_sample = ks_s.reshape(1, db, t, N_HEADS, HEAD_DIM)
    new_v_sample = vs_s.reshape(1, db, t, N_HEADS, HEAD_DIM)
    return (y_prompt, y_sample, new_k_prompt, new_v_prompt, new_state_prompt,
            new_k_sample, new_v_sample, new_state_sample[None])
```

```python
import functools

import numpy as np
import jax
import jax.numpy as jnp
from jax import lax
from jax.experimental import pallas as pl
from jax.experimental.pallas import tpu as pltpu

F32 = jnp.float32
BF16 = jnp.bfloat16

D_MODEL = 1024
HEAD_DIM = 64
N_HEADS = 8
WIDTH = N_HEADS * HEAD_DIM
N_SEG = 8
N_PAIR = N_HEADS // 2
LANES = 128
N_META = 16
CHUNK = 128
ROPE_BASE = 10000.0
EPS = 1e-6
QK_SCALE = HEAD_DIM ** -0.5
LOG2E = 1.4426950408889634
EXP2_MAX = 126.0
VMEM_LIMIT = 56 * 1024 * 1024


def _nt_dot(a, b):
    return lax.dot_general(a, b, (((1,), (1,)), ((), ())), preferred_element_type=F32)


def _tn_dot(a, b):
    return lax.dot_general(a, b, (((0,), (0,)), ((), ())), preferred_element_type=F32)


def _split_bf16(x):
    hi = x.astype(BF16)
    lo = (x - hi.astype(F32)).astype(BF16)
    return jnp.concatenate([hi, lo], axis=1)


def _pair_head_mean(sq, g2):
    return jnp.dot(_split_bf16(sq), g2, preferred_element_type=F32)


def _log2_sigmoids(z2):
    zc = jnp.minimum(z2, EXP2_MAX)
    lk2 = jnp.log(1.0 + jnp.exp2(zc)) * (-LOG2E)
    return lk2 + zc, lk2


def _silu(g):
    return g * (1.0 / (1.0 + jnp.exp(-g)))


def _rope_pair(z, cos, sin_signed):
    lane = lax.broadcasted_iota(jnp.int32, z.shape, 1)
    first_half = (lane % HEAD_DIM) < (HEAD_DIM // 2)
    swapped = jnp.where(first_half,
                        pltpu.roll(z, LANES - HEAD_DIM // 2, axis=1),
                        pltpu.roll(z, HEAD_DIM // 2, axis=1))
    return z * cos + swapped * sin_signed


def _proj_kernel(x_ref, nw_ref, w_ref, cos_ref, sin_ref, qnw_ref, knw_ref, g2_ref,
                 qr_ref, kr_ref, vr_ref, gr_ref, qs_ref, ks_ref, vs_ref, gs_ref,
                 *attn_refs):
    x = x_ref[0]
    ms = jnp.mean(x * x, axis=-1, keepdims=True)
    xn = ((x * lax.rsqrt(ms + EPS)) * nw_ref[...]).astype(BF16)
    cos = cos_ref[...]
    sin = sin_ref[...]
    g2 = g2_ref[...]

    def seg(i):
        return jnp.dot(xn, w_ref[:, i * WIDTH:(i + 1) * WIDTH], preferred_element_type=F32)

    def pair(z, p):
        return z[:, p * LANES:(p + 1) * LANES]

    z = seg(0)
    for p in range(N_PAIR):
        qr_ref[0, :, p * LANES:(p + 1) * LANES] = _rope_pair(pair(z, p), cos, sin).astype(BF16)
    z = seg(1)
    for p in range(N_PAIR):
        kr_ref[0, :, p * LANES:(p + 1) * LANES] = (
            _rope_pair(pair(z, p), cos, sin) * QK_SCALE).astype(BF16)
    vr_ref[0] = seg(2).astype(BF16)
    gr_ref[0] = seg(3)
    z = seg(4)
    for p in range(N_PAIR):
        zp = pair(z, p)
        r = lax.rsqrt(_pair_head_mean(zp * zp, g2) + EPS)
        qs_ref[0, :, p * LANES:(p + 1) * LANES] = (
            (zp * r) * qnw_ref[...] * (QK_SCALE * LOG2E)).astype(BF16)
    z = seg(5)
    for p in range(N_PAIR):
        zp = pair(z, p)
        r = lax.rsqrt(_pair_head_mean(zp * zp, g2) + EPS)
        kn = (zp * r) * knw_ref[...]
        ks_ref[0, :, p * LANES:(p + 1) * LANES] = kn
        if attn_refs:
            kt = kn.T.astype(BF16)
            row = lax.broadcasted_iota(jnp.int32, (LANES, CHUNK), 0)
            zero = jnp.zeros((LANES, CHUNK), BF16)
            for u in range(kt.shape[1] // CHUNK):
                ku = kt[:, u * CHUNK:(u + 1) * CHUNK]
                attn_refs[0][0, p, u, :, :CHUNK] = jnp.where(row < HEAD_DIM, ku, zero)
                attn_refs[0][0, p, u, :, CHUNK:] = jnp.where(row < HEAD_DIM, zero, ku)
    z = seg(6)
    vs_ref[0] = z
    if attn_refs:
        lane = lax.broadcasted_iota(jnp.int32, (CHUNK, LANES), 1)
        zero = jnp.zeros((CHUNK, LANES), BF16)
        for p in range(N_PAIR):
            vb = pair(z, p).astype(BF16)
            for u in range(vb.shape[0] // CHUNK):
                vu = vb[u * CHUNK:(u + 1) * CHUNK, :]
                attn_refs[1][0, p, u, :CHUNK, :] = jnp.where(lane < HEAD_DIM, vu, zero)
                attn_refs[1][0, p, u, CHUNK:, :] = jnp.where(lane < HEAD_DIM, zero, vu)
    gs_ref[0] = seg(7)


def _project(x, norm_w, w_in_bf, cos_t, sin_t, qnw2, knw2, g2, tm, attn_blocks=False):
    b, t, _ = x.shape
    grid = (b, t // tm)
    row = lambda i, j: (i, j, 0)
    const2 = lambda i, j: (0, 0)
    out_block = pl.BlockSpec((1, tm, WIDTH), row)
    f32_out = jax.ShapeDtypeStruct((b, t, WIDTH), F32)
    bf_out = jax.ShapeDtypeStruct((b, t, WIDTH), BF16)
    out_specs = [out_block] * 8
    out_shape = [bf_out, bf_out, bf_out, f32_out, bf_out, f32_out, f32_out, f32_out]
    if attn_blocks:
        nb = tm // CHUNK
        blk5 = lambda i, j: (i, 0, j, 0, 0)
        out_specs += [pl.BlockSpec((1, N_PAIR, nb, LANES, 2 * CHUNK), blk5),
                      pl.BlockSpec((1, N_PAIR, nb, 2 * CHUNK, LANES), blk5)]
        out_shape += [jax.ShapeDtypeStruct((b, N_PAIR, t // CHUNK, LANES, 2 * CHUNK), BF16),
                      jax.ShapeDtypeStruct((b, N_PAIR, t // CHUNK, 2 * CHUNK, LANES), BF16)]
    return pl.pallas_call(
        _proj_kernel,
        grid=grid,
        in_specs=[
            pl.BlockSpec((1, tm, D_MODEL), row),
            pl.BlockSpec((1, D_MODEL), const2),
            pl.BlockSpec((D_MODEL, N_SEG * WIDTH), const2),
            pl.BlockSpec((tm, LANES), lambda i, j: (j, 0)),
            pl.BlockSpec((tm, LANES), lambda i, j: (j, 0)),
            pl.BlockSpec((1, LANES), const2),
            pl.BlockSpec((1, LANES), const2),
            pl.BlockSpec((2 * LANES, LANES), const2),
        ],
        out_specs=out_specs,
        out_shape=out_shape,
        compiler_params=pltpu.CompilerParams(
            dimension_semantics=("arbitrary", "arbitrary"),
            vmem_limit_bytes=VMEM_LIMIT),
        name="proj",
    )(x, norm_w, w_in_bf, cos_t, sin_t, qnw2, knw2, g2)


RET_CHUNKS = 2


def _gate_norm_pair(o, g, w, g2):
    r = lax.rsqrt(_pair_head_mean(o * o, g2) + EPS)
    return ((o * r) * w) * _silu(g)


def _ret_kernel(q_ref, k_ref, v_ref, g_ref, km_ref, vm_ref, dmat_ref, cdec_ref, kdec_ref,
                kdecm_ref, gam_ref, bd_ref, w_ref, g2_ref, y_ref, sout_ref, state_ref):
    c = pl.program_id(1)
    bd = bd_ref[...] > 0.0
    g2 = g2_ref[...]
    lane = lax.broadcasted_iota(jnp.int32, (CHUNK, LANES), 1)
    low = lane < HEAD_DIM

    @pl.when(c == 0)
    def _():
        for p in range(N_PAIR):
            sl = slice(p * LANES, (p + 1) * LANES)
            kd = (km_ref[:, sl].astype(F32) * kdecm_ref[:, sl]).astype(BF16)
            state_ref[p] = jnp.where(bd, _tn_dot(kd, vm_ref[:, sl]), 0.0)

    zero = jnp.zeros((CHUNK, LANES), BF16)
    for u in range(RET_CHUNKS):
        rows = slice(u * CHUNK, (u + 1) * CHUNK)
        vs, sc, cross, upd, old = [], [], [], [], []
        for p in range(N_PAIR):
            sl = slice(p * LANES, (p + 1) * LANES)
            q = q_ref[0, rows, sl]
            k = k_ref[0, rows, sl]
            v = v_ref[0, rows, sl]
            s = state_ref[p]
            sc.append([_nt_dot(jnp.where(low, q, zero), k), _nt_dot(jnp.where(low, zero, q), k)])
            cross.append(jnp.dot(q, s.astype(BF16), preferred_element_type=F32))
            kd = (k.astype(F32) * kdec_ref[:, sl]).astype(BF16)
            upd.append(_tn_dot(kd, v))
            vs.append(v)
            old.append(s)
        for p in range(N_PAIR):
            state_ref[p] = gam_ref[p] * old[p] + jnp.where(bd, upd[p], 0.0)
            sc[p] = [(sc[p][e] * dmat_ref[2 * p + e]).astype(BF16) for e in range(2)]
        inner = [[jnp.dot(sc[p][e], vs[p], preferred_element_type=F32) for e in range(2)]
                 for p in range(N_PAIR)]
        for p in range(N_PAIR):
            sl = slice(p * LANES, (p + 1) * LANES)
            o = jnp.where(low, inner[p][0], inner[p][1]) + cross[p] * cdec_ref[:, sl]
            y_ref[0, rows, sl] = _gate_norm_pair(o, g_ref[0, rows, sl], w_ref[:, sl],
                                                 g2).astype(BF16)

    @pl.when(c == pl.num_programs(1) - 1)
    def _():
        sout_ref[0] = state_ref[...]


def _retention_prompt(qr, kr, vr, gr, kr_meta, vr_meta, tabs, w_onorm, g2):
    b, t, _ = qr.shape
    row = lambda i, j: (i, j, 0)
    c2 = lambda i, j: (0, 0)
    c3 = lambda i, j: (0, 0, 0)
    blk = pl.BlockSpec((1, RET_CHUNKS * CHUNK, WIDTH), row)
    return pl.pallas_call(
        _ret_kernel,
        grid=(b, t // (RET_CHUNKS * CHUNK)),
        in_specs=[
            blk, blk, blk, blk,
            pl.BlockSpec((N_META, WIDTH), c2),
            pl.BlockSpec((N_META, WIDTH), c2),
            pl.BlockSpec((N_HEADS, CHUNK, CHUNK), c3),
            pl.BlockSpec((CHUNK, WIDTH), c2),
            pl.BlockSpec((CHUNK, WIDTH), c2),
            pl.BlockSpec((N_META, WIDTH), c2),
            pl.BlockSpec((N_PAIR, LANES, LANES), c3),
            pl.BlockSpec((LANES, LANES), c2),
            pl.BlockSpec((1, WIDTH), c2),
            pl.BlockSpec((2 * LANES, LANES), c2),
        ],
        out_specs=[blk, pl.BlockSpec((1, N_PAIR, LANES, LANES), lambda i, j: (i, 0, 0, 0))],
        out_shape=[jax.ShapeDtypeStruct((b, t, WIDTH), BF16),
                   jax.ShapeDtypeStruct((b, N_PAIR, LANES, LANES), F32)],
        scratch_shapes=[pltpu.VMEM((N_PAIR, LANES, LANES), F32)],
        compiler_params=pltpu.CompilerParams(
            dimension_semantics=("arbitrary", "arbitrary"),
            vmem_limit_bytes=VMEM_LIMIT),
        name="ret_prompt",
    )(qr, kr, vr, gr, kr_meta, vr_meta, tabs["dmat"], tabs["cdec"], tabs["kdec"],
      tabs["kdec_meta"], tabs["gam"], tabs["bd"], w_onorm, g2)


SB_Q = 2 * CHUNK
SB_KB = 2
SB_SPLIT = 4


def _sb_step(q_ref, blocks, bias_ref, tt, acc_ref, carry_ref, side=None):
    def mask(vis_of, width):
        key = lax.broadcasted_iota(jnp.int32, (SB_Q, width), 1) % LANES
        return vis_of(lax.broadcasted_iota(jnp.int32, (SB_Q, width), 0), key)

    side = side or [lambda: None] * 3
    side[0]()
    ls_all, lkb_all, tot_all = [], [], []
    for kt_of, _, vis_of in blocks:
        for p in range(N_PAIR):
            z = jnp.dot(q_ref[0, :, p * LANES:(p + 1) * LANES], kt_of(p),
                        preferred_element_type=F32) + bias_ref[p]
            ls, lk = _log2_sigmoids(z)
            if vis_of is not None:
                lk = jnp.where(mask(vis_of, 2 * LANES), lk, 0.0)
            ls_all.append(ls)
            lkb_all.append(lk.astype(BF16))
            tot_all.append([jnp.sum(lk[:, e * LANES:(e + 1) * LANES], axis=1, keepdims=True)
                            for e in range(2)])
    side[1]()
    cc_all = [jnp.dot(lkb, tt, preferred_element_type=F32) for lkb in lkb_all]
    side[2]()
    for p in range(N_PAIR):
        a_run = []
        for n, (_, _, vis_of) in enumerate(blocks):
            s = ls_all[n * N_PAIR + p] + cc_all[n * N_PAIR + p]
            for e in range(2):
                h = 2 * p + e
                a = jnp.exp2(s[:, e * LANES:(e + 1) * LANES] + carry_ref[h])
                if vis_of is not None:
                    a = jnp.where(mask(vis_of, LANES), a, 0.0)
                a_run.append(a.astype(BF16))
                carry_ref[h] += tot_all[n * N_PAIR + p][e]
        acc_ref[p] += jnp.dot(jnp.concatenate(a_run, axis=1),
                              jnp.concatenate([v_of(p) for _, v_of, _ in blocks], axis=0),
                              preferred_element_type=F32)


def _sb_prompt_part(k, q_ref, kt_ref, v_ref, ktm_ref, vm_ref, bias_ref, g_ref, w_ref, tt, g2,
                    y_ref, acc_ref, carry_ref, make_side):
    i = pl.program_id(1)

    def cached(blk, vis_of):
        return (lambda p: kt_ref[0, p, blk], lambda p: v_ref[0, p, blk], vis_of)

    @pl.when(k == 0)
    def _():
        acc_ref[...] = jnp.zeros_like(acc_ref)
        carry_ref[...] = jnp.zeros_like(carry_ref)
        _sb_step(q_ref,
                 [cached(2 * i + d, lambda r, s, d=d: s < r - d * CHUNK) for d in (1, 0)],
                 bias_ref, tt, acc_ref, carry_ref, make_side())

    def iteration(jj, side=None):
        newest = 2 * i - 1 - SB_KB * jj
        _sb_step(q_ref, [cached(newest - n, None) for n in range(SB_KB)],
                 bias_ref, tt, acc_ref, carry_ref, side)

    lo = (k * i) // SB_SPLIT
    hi = ((k + 1) * i) // SB_SPLIT
    between = (k > 0) & (k < SB_SPLIT - 1)

    @pl.when(between & (hi > lo))
    def _():
        iteration(lo, make_side())

    @pl.when(between & (hi == lo))
    def _():
        for stage in make_side():
            stage()

    def body(jj, carry):
        iteration(jj)
        return carry

    lax.fori_loop(lo + jnp.where(between, 1, 0), hi, body, 0)

    @pl.when(k == SB_SPLIT - 1)
    def _():
        _sb_step(q_ref, [(lambda p: ktm_ref[p], lambda p: vm_ref[p], lambda r, s: s < N_META)],
                 bias_ref, tt, acc_ref, carry_ref, make_side())
        for p in range(N_PAIR):
            sl = slice(p * LANES, (p + 1) * LANES)
            y_ref[0, :, sl] = _gate_norm_pair(acc_ref[p], g_ref[0, :, sl], w_ref[:, sl],
                                              g2).astype(BF16)


def _out_kernel(x_ref, yr_ref, ys_ref, w_ref, o_ref):
    o_ref[0] = (x_ref[0]
                + jnp.dot(yr_ref[0], w_ref[:WIDTH, :], preferred_element_type=F32)
                + jnp.dot(ys_ref[0], w_ref[WIDTH:, :], preferred_element_type=F32))


def _out_proj(x, yr, ys, w_out_bf, tm):
    b, t, _ = x.shape
    row = lambda i, j: (i, j, 0)
    return pl.pallas_call(
        _out_kernel,
        grid=(b, t // tm),
        in_specs=[
            pl.BlockSpec((1, tm, D_MODEL), row),
            pl.BlockSpec((1, tm, WIDTH), row),
            pl.BlockSpec((1, tm, WIDTH), row),
            pl.BlockSpec((2 * WIDTH, D_MODEL), lambda i, j: (0, 0)),
        ],
        out_specs=pl.BlockSpec((1, tm, D_MODEL), row),
        out_shape=jax.ShapeDtypeStruct(x.shape, F32),
        compiler_params=pltpu.CompilerParams(
            dimension_semantics=("arbitrary", "arbitrary"),
            vmem_limit_bytes=VMEM_LIMIT),
        name="out_proj",
    )(x, yr, ys, w_out_bf)


def _ret_sample_kernel(q_ref, k_ref, v_ref, g_ref, s_ref, dmat_ref, cdec_ref, kdec_ref,
                       gam_ref, w_ref, y_ref, sout_ref):
    n = q_ref.shape[0] * N_HEADS
    t = q_ref.shape[2]
    q = q_ref[...].reshape(n, t, HEAD_DIM)
    k = k_ref[...].reshape(n, t, HEAD_DIM)
    v = v_ref[...].reshape(n, t, HEAD_DIM)
    s = s_ref[...].reshape(n, HEAD_DIM, HEAD_DIM)
    sc = jnp.einsum("ntd,nsd->nts", q, k, preferred_element_type=F32) * dmat_ref[...]
    inner = jnp.einsum("nts,nse->nte", sc, v, preferred_element_type=F32)
    cross = jnp.einsum("ntd,nde->nte", q, s, preferred_element_type=F32) * cdec_ref[...]
    o = inner + cross
    upd = jnp.einsum("ntd,nte->nde", k * kdec_ref[...], v, preferred_element_type=F32)
    sout_ref[...] = (gam_ref[...] * s + upd).reshape(sout_ref.shape)
    r = lax.rsqrt(jnp.mean(o * o, axis=-1, keepdims=True) + EPS)
    g = g_ref[...].reshape(n, t, HEAD_DIM)
    y_ref[...] = (((o * r) * w_ref[...]) * _silu(g)).reshape(y_ref.shape)


def _retention_sample(q4, k4, v4, g4, state, tabs, w_heads, sblk):
    db, _, t, _ = q4.shape
    blk = pl.BlockSpec((sblk, N_HEADS, t, HEAD_DIM), lambda i: (i, 0, 0, 0))
    sblock = pl.BlockSpec((sblk, N_HEADS, HEAD_DIM, HEAD_DIM), lambda i: (i, 0, 0, 0))
    c3 = lambda i: (0, 0, 0)
    n = sblk * N_HEADS
    return pl.pallas_call(
        _ret_sample_kernel,
        grid=(db // sblk,),
        in_specs=[blk, blk, blk, blk, sblock,
                  pl.BlockSpec((n, t, t), c3),
                  pl.BlockSpec((n, t, 1), c3),
                  pl.BlockSpec((n, t, 1), c3),
                  pl.BlockSpec((n, 1, 1), c3),
                  pl.BlockSpec((n, 1, HEAD_DIM), c3)],
        out_specs=[blk, sblock],
        out_shape=[jax.ShapeDtypeStruct(q4.shape, F32),
                   jax.ShapeDtypeStruct(state.shape, F32)],
        compiler_params=pltpu.CompilerParams(
            dimension_semantics=("arbitrary",),
            vmem_limit_bytes=VMEM_LIMIT),
        name="ret_sample",
    )(q4, k4, v4, g4, state, tabs["dmat"], tabs["cdec"], tabs["kdec"], tabs["gam"], w_heads)


def _sample_stages(qblk, bias, k_refs, v_refs, tt, acc_ref, carry_ref, vis=None, first=False):
    n = len(k_refs)
    rows = qblk.shape[0]
    st = {}

    def scores():
        kts = [r[...].astype(BF16) for r in k_refs]
        st["z"] = jnp.dot(qblk, jnp.concatenate(kts, axis=1), preferred_element_type=F32)

    def keep_sums():
        ls, lk = _log2_sigmoids(st["z"] + jnp.concatenate([bias] * n, axis=1))
        if vis is not None:
            lk = jnp.where(vis, lk, 0.0)
        lk_rows = jnp.concatenate([lk[:, g * LANES:(g + 1) * LANES] for g in range(n)], axis=0)
        st["ls"] = ls
        st["cc"] = jnp.dot(_split_bf16(lk_rows), tt, preferred_element_type=F32)

    def weights():
        carry = jnp.zeros((rows, LANES), F32) if first else carry_ref[...]
        a_all = []
        for g in range(n):
            ccg = st["cc"][g * rows:(g + 1) * rows]
            a = jnp.exp2(st["ls"][:, g * LANES:(g + 1) * LANES] + ccg[:, :LANES] + carry)
            if vis is not None:
                a = jnp.where(vis, a, 0.0)
            a_all.append(a.astype(BF16))
            carry = carry + ccg[:, LANES:]
        a_cat = jnp.concatenate(a_all, axis=1)
        a_pad = jnp.concatenate([a_cat, jnp.zeros((LANES - rows, a_cat.shape[1]), BF16)], axis=0)
        vts = [r[...].astype(BF16) for r in v_refs]
        av = _nt_dot(jnp.concatenate(vts, axis=1), a_pad)
        acc_ref[...] = av if first else acc_ref[...] + av
        carry_ref[...] = carry

    return [scores, keep_sums, weights]


def _sample_new_tokens(qblk_ref, bias_ref, ktn_ref, vtn_ref, tt, t, acc_ref, carry_ref):
    rows = qblk_ref.shape[1]
    rr = lax.broadcasted_iota(jnp.int32, (rows, LANES), 0)
    col = lax.broadcasted_iota(jnp.int32, (rows, LANES), 1)
    for stage in _sample_stages(qblk_ref[0], bias_ref[...], [ktn_ref.at[0]], [vtn_ref.at[0]], tt,
                                acc_ref, carry_ref, vis=col < (rr % t), first=True):
        stage()


def _sample_finish(g_ref, w_ref, g2, y_ref, acc_ref):
    t = y_ref.shape[1]
    acc = acc_ref[...].T
    lane = lax.broadcasted_iota(jnp.int32, (t, WIDTH), 1)
    o = jnp.zeros((t, WIDTH), F32)
    for h in range(N_HEADS):
        o = o + jnp.where(lane // HEAD_DIM == h, acc[h * t:(h + 1) * t, :], 0.0)
    for p in range(N_PAIR):
        sl = slice(p * LANES, (p + 1) * LANES)
        y_ref[0, :, sl] = _gate_norm_pair(o[:, sl], g_ref[0, :, sl], w_ref[:, sl],
                                          g2).astype(BF16)


N_PROMPT_IN = 8
N_SAMPLE_IN = 6


def _page_copies(pt_ref, kt_hbm, vt_hbm, kbuf, vbuf, sem, step, n_group):
    n_pages = pt_ref.shape[1]
    slot = step % 2
    seq = step // SB_SPLIT
    newest = n_pages - 1 - (step % SB_SPLIT) * n_group
    copies = []
    for g in range(n_group):
        pid = pt_ref[seq, newest - g]
        copies.append(pltpu.make_async_copy(kt_hbm.at[pid], kbuf.at[slot, g], sem.at[slot, 0]))
        copies.append(pltpu.make_async_copy(vt_hbm.at[pid], vbuf.at[slot, g], sem.at[slot, 1]))
    return copies


def _sb_fused_kernel(pt_ref, w_ref, g2_ref, *rest):
    p_in = rest[:N_PROMPT_IN]
    s_in = rest[N_PROMPT_IN:N_PROMPT_IN + N_SAMPLE_IN]
    kt_hbm, vt_hbm = rest[N_PROMPT_IN + N_SAMPLE_IN:N_PROMPT_IN + N_SAMPLE_IN + 2]
    yp_ref, ys_ref, acc_ref, carry_ref, sacc_ref, scarry_ref, kbuf, vbuf, sem = rest[-9:]
    q_ref, kt_ref, v_ref, ktm_ref, vm_ref, bias_ref, g_ref, ttp_ref = p_in
    qblk_ref, sbias_ref, ktn_ref, vtn_ref, sg_ref, tts_ref = s_in
    n_group = kbuf.shape[1]
    k = pl.program_id(2)
    step = (pl.program_id(0) * pl.num_programs(1) + pl.program_id(1)) * SB_SPLIT + k
    n_steps = pl.num_programs(0) * pl.num_programs(1) * SB_SPLIT

    def copies(s):
        return _page_copies(pt_ref, kt_hbm, vt_hbm, kbuf, vbuf, sem, s, n_group)

    @pl.when(step == 0)
    def _():
        for c in copies(step):
            c.start()

    @pl.when(step + 1 < n_steps)
    def _():
        for c in copies(step + 1):
            c.start()

    for c in copies(step):
        c.wait()

    slot = step % 2
    g2 = g2_ref[...]
    tts = tts_ref[...]
    last = SB_SPLIT - 1

    def page_stages():
        return _sample_stages(qblk_ref[0], sbias_ref[...],
                              [kbuf.at[slot, g] for g in range(n_group)],
                              [vbuf.at[slot, g] for g in range(n_group)], tts, sacc_ref, scarry_ref)

    @pl.when(k == 0)
    def _():
        _sample_new_tokens(qblk_ref, sbias_ref, ktn_ref, vtn_ref, tts, ys_ref.shape[1],
                           sacc_ref, scarry_ref)

    _sb_prompt_part(k, q_ref, kt_ref, v_ref, ktm_ref, vm_ref, bias_ref, g_ref, w_ref,
                    ttp_ref[...], g2, yp_ref, acc_ref, carry_ref, page_stages)

    @pl.when(k == last)
    def _():
        _sample_finish(sg_ref, w_ref, g2, ys_ref, sacc_ref)


def _sb_fused(page_table, w_onorm, g2, qs, kt_bd, v_bd, ktm_bd, vm_bd, bias_pairs, gs, ttp,
              qblk, bias_rows, kt_new, vt_new, gs_s, tts, cache_kt, cache_vt):
    b, t, _ = qs.shape
    nblk = t // CHUNK
    n_q = t // SB_Q
    db, n_pages = page_table.shape
    ts = gs_s.shape[1]
    rows = qblk.shape[1]
    page = cache_kt.shape[2]
    n_group = n_pages // SB_SPLIT
    assert db == b * n_q and n_pages % SB_SPLIT == 0

    c2 = lambda i, j, k, pt: (0, 0)
    c3 = lambda i, j, k, pt: (0, 0, 0)
    qrow = lambda i, j, k, pt: (i, j, 0)
    batch5 = lambda i, j, k, pt: (i, 0, 0, 0, 0)
    seq3 = lambda i, j, k, pt: (i * n_q + j, 0, 0)
    qblock = pl.BlockSpec((1, SB_Q, WIDTH), qrow)
    once = pl.Buffered(1)
    grid_spec = pltpu.PrefetchScalarGridSpec(
        num_scalar_prefetch=1,
        grid=(b, n_q, SB_SPLIT),
        in_specs=[
            pl.BlockSpec((1, WIDTH), c2),
            pl.BlockSpec((2 * LANES, LANES), c2),
            qblock,
            pl.BlockSpec((1, N_PAIR, nblk, LANES, 2 * CHUNK), batch5, pipeline_mode=once),
            pl.BlockSpec((1, N_PAIR, nblk, 2 * CHUNK, LANES), batch5, pipeline_mode=once),
            pl.BlockSpec((N_PAIR, LANES, 2 * CHUNK), c3),
            pl.BlockSpec((N_PAIR, 2 * CHUNK, LANES), c3),
            pl.BlockSpec((N_PAIR, 1, 2 * LANES), c3),
            qblock,
            pl.BlockSpec((2 * LANES, 2 * LANES), c2),
            pl.BlockSpec((1, rows, WIDTH), seq3),
            pl.BlockSpec((rows, LANES), c2),
            pl.BlockSpec((1, WIDTH, LANES), seq3),
            pl.BlockSpec((1, WIDTH, LANES), seq3),
            pl.BlockSpec((1, ts, WIDTH), seq3),
            pl.BlockSpec((2 * LANES, 2 * LANES), c2),
            pl.BlockSpec(memory_space=pl.ANY),
            pl.BlockSpec(memory_space=pl.ANY),
        ],
        out_specs=[qblock, pl.BlockSpec((1, ts, WIDTH), seq3)],
        scratch_shapes=[pltpu.VMEM((N_PAIR, SB_Q, LANES), F32),
                        pltpu.VMEM((N_HEADS, SB_Q, LANES), F32),
                        pltpu.VMEM((WIDTH, LANES), F32),
                        pltpu.VMEM((rows, LANES), F32),
                        pltpu.VMEM((2, n_group, WIDTH, page), F32),
                        pltpu.VMEM((2, n_group, WIDTH, page), F32),
                        pltpu.SemaphoreType.DMA((2, 2))],
    )
    return pl.pallas_call(
        _sb_fused_kernel,
        grid_spec=grid_spec,
        out_shape=[jax.ShapeDtypeStruct((b, t, WIDTH), BF16),
                   jax.ShapeDtypeStruct((db, ts, WIDTH), BF16)],
        compiler_params=pltpu.CompilerParams(
            dimension_semantics=("arbitrary", "arbitrary", "arbitrary"),
            vmem_limit_bytes=VMEM_LIMIT),
        name="sb_fused",
    )(page_table, w_onorm, g2, qs, kt_bd, v_bd, ktm_bd, vm_bd, bias_pairs, gs, ttp,
      qblk, bias_rows, kt_new, vt_new, gs_s, tts, cache_kt, cache_vt)


def _rope_tables(pos):
    half = HEAD_DIM // 2
    inv = ROPE_BASE ** (-jnp.arange(half, dtype=F32) / half)
    ang = pos.astype(F32)[:, None] * inv[None, :]
    cos = jnp.tile(jnp.cos(ang), (1, LANES // half))
    sin = jnp.sin(ang)
    sin_signed = jnp.tile(jnp.concatenate([-sin, sin], axis=1), (1, LANES // HEAD_DIM))
    return cos, sin_signed


def _log_gamma():
    return jnp.log1p(-jnp.exp2(-5.0 - jnp.arange(N_HEADS, dtype=F32)))


def _head_lanes(x):
    return jnp.repeat(x, HEAD_DIM, axis=-1)


def _retention_tables_prompt():
    lg = _log_gamma()
    idx = jnp.arange(CHUNK, dtype=F32)
    rel = idx[:, None] - idx[None, :]
    dmat = jnp.where(rel[None] >= 0, jnp.exp(jnp.maximum(rel, 0.0)[None] * lg[:, None, None]), 0.0)
    cdec = _head_lanes(jnp.exp((idx + 1.0)[:, None] * lg[None, :]))
    kdec = _head_lanes(jnp.exp((CHUNK - 1.0 - idx)[:, None] * lg[None, :]))
    midx = jnp.arange(N_META, dtype=F32)
    kdec_meta = _head_lanes(jnp.exp((N_META - 1.0 - midx)[:, None] * lg[None, :]))
    head_of = jnp.arange(LANES) // HEAD_DIM
    bd = (head_of[:, None] == head_of[None, :]).astype(F32)
    gfull = jnp.exp(CHUNK * lg).reshape(N_PAIR, 2)
    gam = bd[None] * gfull[:, head_of][:, :, None]
    return {"dmat": dmat, "cdec": cdec, "kdec": kdec, "kdec_meta": kdec_meta, "gam": gam, "bd": bd}


def _retention_tables_sample(t, sblk):
    lg = _log_gamma()
    idx = jnp.arange(t, dtype=F32)
    rel = idx[:, None] - idx[None, :]
    dmat = jnp.where(rel[None] >= 0, jnp.exp(jnp.maximum(rel, 0.0)[None] * lg[:, None, None]), 0.0)
    cdec = jnp.exp((idx + 1.0)[None, :] * lg[:, None])[:, :, None]
    kdec = jnp.exp((t - 1.0 - idx)[None, :] * lg[:, None])[:, :, None]
    gam = jnp.exp(t * lg)[:, None, None]
    rep = lambda a: jnp.tile(a, (sblk, 1, 1))
    return {"dmat": rep(dmat), "cdec": rep(cdec), "kdec": rep(kdec), "gam": rep(gam)}


def _pair_mean_matrix():
    head_of = np.arange(LANES) // HEAD_DIM
    g = (head_of[:, None] == head_of[None, :]).astype(np.float32) / HEAD_DIM
    return jnp.asarray(np.concatenate([g, g], axis=0), dtype=BF16)


def _pair_suffix_sum_matrix():
    i = np.arange(LANES)
    tri = (i[:, None] > i[None, :]).astype(np.float32)
    zero = np.zeros((LANES, LANES), np.float32)
    return jnp.asarray(np.block([[tri, zero], [zero, tri]]), dtype=BF16)


def _suffix_sum_matrix_lanes():
    i = np.arange(LANES)
    tri = (i[:, None] > i[None, :]).astype(np.float32)
    half = np.concatenate([tri, np.ones((LANES, LANES), np.float32)], axis=1)
    return jnp.asarray(np.concatenate([half, half], axis=0), dtype=BF16)


def kernel(x_prompt, x_sample, cache_k, cache_v, state_ret, page_table, meta_tokens, norm_w, w_in,
           q_norm_w, k_norm_w, sb_bias, ret_onorm_w, sb_onorm_w, w_out):
    b, seq, _ = x_prompt.shape
    db, t, _ = x_sample.shape
    n_pool, page = cache_k.shape[1], cache_k.shape[2]
    n_pages = page_table.shape[1]
    past_len = n_pages * page

    w_in_bf = w_in[0].astype(BF16)
    w_out_bf = w_out[0].astype(BF16)
    nw = norm_w[0][None, :]
    qnw2 = jnp.tile(q_norm_w[0], LANES // HEAD_DIM)[None, :]
    knw2 = jnp.tile(k_norm_w[0], LANES // HEAD_DIM)[None, :]
    w_ret = ret_onorm_w[0][None, :]
    w_sb = sb_onorm_w[0][None, :]
    bias = sb_bias[0] * LOG2E
    g2 = _pair_mean_matrix()
    tt = _suffix_sum_matrix_lanes()

    cos_m, sin_m = _rope_tables(jnp.arange(N_META))
    cos_p, sin_p = _rope_tables(N_META + jnp.arange(seq))
    _, kr_m, vr_m, _, _, ks_m, vs_m, _ = _project(
        meta_tokens[None], nw, w_in_bf, cos_m, sin_m, qnw2, knw2, g2, N_META)
    qr, kr, vr, gr, qs, ks, vs, gs, kt_bd, v_bd = _project(
        x_prompt, nw, w_in_bf, cos_p, sin_p, qnw2, knw2, g2, 256, attn_blocks=True)

    tabs = _retention_tables_prompt()
    y_r, s_pairs = _retention_prompt(qr, kr, vr, gr, kr_m[0], vr_m[0], tabs, w_ret, g2)

    pad = ((0, CHUNK - N_META), (0, 0))
    km = jnp.pad(ks_m[0], pad).astype(BF16).reshape(CHUNK, N_PAIR, 2, HEAD_DIM)
    vm = jnp.pad(vs_m[0], pad).astype(BF16).reshape(CHUNK, N_PAIR, 2, HEAD_DIM)
    eye2 = jnp.eye(2, dtype=BF16)
    ktm_bd = (km.transpose(1, 2, 3, 0)[:, :, :, None, :] * eye2[None, :, None, :, None]
              ).reshape(N_PAIR, LANES, 2 * CHUNK)
    vm_bd = (vm.transpose(1, 0, 2, 3)[:, None, :, :, :] * eye2[None, :, None, :, None]
             ).reshape(N_PAIR, 2 * CHUNK, LANES)
    bias_pairs = jnp.repeat(bias.reshape(N_PAIR, 1, 2), LANES, axis=2)

    def with_meta(m, a):
        full = jnp.concatenate([jnp.broadcast_to(m, (b, N_META, WIDTH)), a], axis=1)
        return full.reshape(1, b, seq + N_META, N_HEADS, HEAD_DIM)

    new_k_prompt = with_meta(ks_m, ks)
    new_v_prompt = with_meta(vs_m, vs)
    sp = s_pairs.reshape(b, N_PAIR, 2, HEAD_DIM, 2, HEAD_DIM)
    new_state_prompt = jnp.stack([sp[:, :, 0, :, 0, :], sp[:, :, 1, :, 1, :]], axis=2)
    new_state_prompt = new_state_prompt.reshape(1, b, N_HEADS, HEAD_DIM, HEAD_DIM)

    n_tok = db * t
    cos_s, sin_s = _rope_tables(jnp.tile(past_len + jnp.arange(t), db))
    qr_s, kr_s, vr_s, gr_s, qs_s, ks_s, vs_s, gs_s = _project(
        x_sample.reshape(1, n_tok, D_MODEL), nw, w_in_bf, cos_s, sin_s, qnw2, knw2, g2, 256)

    def heads_major(a):
        return a.reshape(db, t, N_HEADS, HEAD_DIM).transpose(0, 2, 1, 3).astype(F32)

    sblk = 8
    tabs_s = _retention_tables_sample(t, sblk)
    w_heads = jnp.tile(ret_onorm_w[0].reshape(N_HEADS, 1, HEAD_DIM), (sblk, 1, 1))
    y_r4, new_state_sample = _retention_sample(
        heads_major(qr_s), heads_major(kr_s), heads_major(vr_s), heads_major(gr_s),
        state_ret[0], tabs_s, w_heads, sblk)
    y_r_s = y_r4.transpose(0, 2, 1, 3).reshape(1, n_tok, WIDTH).astype(BF16)

    q4 = qs_s.reshape(db, t, N_HEADS, HEAD_DIM).transpose(0, 2, 1, 3)
    eye = jnp.eye(N_HEADS, dtype=BF16)
    qblk = (q4[:, :, :, None, :] * eye[None, :, None, :, None]).reshape(db, N_HEADS * t, WIDTH)
    bias_rows = jnp.broadcast_to(jnp.repeat(bias, t)[:, None], (N_HEADS * t, LANES))

    def new_tokens_t(a):
        at = a.reshape(db, t, WIDTH).transpose(0, 2, 1).astype(BF16)
        return jnp.pad(at, ((0, 0), (0, 0), (0, LANES - t)))

    def pages_t(c):
        return c[0].transpose(0, 2, 3, 1).reshape(n_pool, WIDTH, page)

    y_s, y_s_s = _sb_fused(page_table, w_sb, g2,
                           qs, kt_bd, v_bd, ktm_bd, vm_bd, bias_pairs, gs,
                           _pair_suffix_sum_matrix(),
                           qblk, bias_rows, new_tokens_t(ks_s[0]), new_tokens_t(vs_s[0]),
                           gs_s.reshape(db, t, WIDTH), tt, pages_t(cache_k), pages_t(cache_v))
    y_prompt = _out_proj(x_prompt, y_r, y_s, w_out_bf, 512)
    y_sample = _out_proj(x_sample.reshape(1, n_tok, D_MODEL), y_r_s,
                         y_s_s.reshape(1, n_tok, WIDTH), w_out_bf, 512)
    y_sample = y_sample.reshape(db, t, D_MODEL)

    new_k_sample = ks_s.reshape(1, db, t, N_HEADS, HEAD_DIM)
    new_v_sample = vs_s.reshape(1, db, t, N_HEADS, HEAD_DIM)
    return (y_prompt, y_sample, new_k_prompt, new_v_prompt, new_state_prompt,
            new_k_sample, new_v_sample, new_state_sample[None])
```

```python
import functools

import numpy as np
import jax
import jax.numpy as jnp
from jax import lax
from jax.experimental import pallas as pl
from jax.experimental.pallas import tpu as pltpu

F32 = jnp.float32
BF16 = jnp.bfloat16

D_MODEL = 1024
HEAD_DIM = 64
N_HEADS = 8
WIDTH = N_HEADS * HEAD_DIM
N_SEG = 8
N_PAIR = N_HEADS // 2
LANES = 128
N_META = 16
CHUNK = 128
ROPE_BASE = 10000.0
EPS = 1e-6
QK_SCALE = HEAD_DIM ** -0.5
LOG2E = 1.4426950408889634
EXP2_MAX = 126.0
VMEM_LIMIT = 56 * 1024 * 1024


def _nt_dot(a, b):
    return lax.dot_general(a, b, (((1,), (1,)), ((), ())), preferred_element_type=F32)


def _tn_dot(a, b):
    return lax.dot_general(a, b, (((0,), (0,)), ((), ())), preferred_element_type=F32)


def _split_bf16(x):
    hi = x.astype(BF16)
    lo = (x - hi.astype(F32)).astype(BF16)
    return jnp.concatenate([hi, lo], axis=1)


def _pair_head_mean(sq, g2):
    return jnp.dot(_split_bf16(sq), g2, preferred_element_type=F32)


def _log2_sigmoids(z2):
    zc = jnp.minimum(z2, EXP2_MAX)
    lk2 = jnp.log(1.0 + jnp.exp2(zc)) * (-LOG2E)
    return lk2 + zc, lk2


def _silu(g):
    return g * (1.0 / (1.0 + jnp.exp(-g)))


def _rope_pair(z, cos, sin_signed):
    lane = lax.broadcasted_iota(jnp.int32, z.shape, 1)
    first_half = (lane % HEAD_DIM) < (HEAD_DIM // 2)
    swapped = jnp.where(first_half,
                        pltpu.roll(z, LANES - HEAD_DIM // 2, axis=1),
                        pltpu.roll(z, HEAD_DIM // 2, axis=1))
    return z * cos + swapped * sin_signed


def _proj_kernel(x_ref, nw_ref, w_ref, cos_ref, sin_ref, qnw_ref, knw_ref, g2_ref,
                 qr_ref, kr_ref, vr_ref, gr_ref, qs_ref, ks_ref, vs_ref, gs_ref,
                 *attn_refs):
    x = x_ref[0]
    ms = jnp.mean(x * x, axis=-1, keepdims=True)
    xn = ((x * lax.rsqrt(ms + EPS)) * nw_ref[...]).astype(BF16)
    cos = cos_ref[...]
    sin = sin_ref[...]
    g2 = g2_ref[...]

    def seg(i):
        return jnp.dot(xn, w_ref[:, i * WIDTH:(i + 1) * WIDTH], preferred_element_type=F32)

    def pair(z, p):
        return z[:, p * LANES:(p + 1) * LANES]

    z = seg(0)
    for p in range(N_PAIR):
        qr_ref[0, :, p * LANES:(p + 1) * LANES] = _rope_pair(pair(z, p), cos, sin).astype(BF16)
    z = seg(1)
    for p in range(N_PAIR):
        kr_ref[0, :, p * LANES:(p + 1) * LANES] = (
            _rope_pair(pair(z, p), cos, sin) * QK_SCALE).astype(BF16)
    vr_ref[0] = seg(2).astype(BF16)
    gr_ref[0] = seg(3)
    z = seg(4)
    for p in range(N_PAIR):
        zp = pair(z, p)
        r = lax.rsqrt(_pair_head_mean(zp * zp, g2) + EPS)
        qs_ref[0, :, p * LANES:(p + 1) * LANES] = (
            (zp * r) * qnw_ref[...] * (QK_SCALE * LOG2E)).astype(BF16)
    z = seg(5)
    for p in range(N_PAIR):
        zp = pair(z, p)
        r = lax.rsqrt(_pair_head_mean(zp * zp, g2) + EPS)
        kn = (zp * r) * knw_ref[...]
        ks_ref[0, :, p * LANES:(p + 1) * LANES] = kn
        if attn_refs:
            kt = kn.T.astype(BF16)
            row = lax.broadcasted_iota(jnp.int32, (LANES, CHUNK), 0)
            zero = jnp.zeros((LANES, CHUNK), BF16)
            for u in range(kt.shape[1] // CHUNK):
                ku = kt[:, u * CHUNK:(u + 1) * CHUNK]
                attn_refs[0][0, p, u, :, :CHUNK] = jnp.where(row < HEAD_DIM, ku, zero)
                attn_refs[0][0, p, u, :, CHUNK:] = jnp.where(row < HEAD_DIM, zero, ku)
    z = seg(6)
    vs_ref[0] = z
    if attn_refs:
        lane = lax.broadcasted_iota(jnp.int32, (CHUNK, LANES), 1)
        zero = jnp.zeros((CHUNK, LANES), BF16)
        for p in range(N_PAIR):
            vb = pair(z, p).astype(BF16)
            for u in range(vb.shape[0] // CHUNK):
                vu = vb[u * CHUNK:(u + 1) * CHUNK, :]
                attn_refs[1][0, p, u, :CHUNK, :] = jnp.where(lane < HEAD_DIM, vu, zero)
                attn_refs[1][0, p, u, CHUNK:, :] = jnp.where(lane < HEAD_DIM, zero, vu)
    gs_ref[0] = seg(7)


def _proj_prompt_kernel(*refs):
    ins, (km_ref, vm_ref) = refs[:8], refs[8:10]
    qr, kr, vr, gr, qs, kfull, vfull, gs, ktbd, vbd = refs[10:20]
    kstage, vstage, sem, msem = refs[20:]
    i, j = pl.program_id(0), pl.program_id(1)
    nj = pl.num_programs(1)
    step = i * nj + j
    last = pl.num_programs(0) * nj - 1
    tm = kstage.shape[1]
    n_meta = km_ref.shape[0]

    def copies(s):
        slot = s % 2
        rows = pl.ds(n_meta + (s % nj) * tm, tm)
        return [pltpu.make_async_copy(kstage.at[slot], kfull.at[s // nj, rows, :], sem.at[slot, 0]),
                pltpu.make_async_copy(vstage.at[slot], vfull.at[s // nj, rows, :], sem.at[slot, 1])]

    @pl.when(step >= 2)
    def _():
        for c in copies(step - 2):
            c.wait()

    slot = step % 2
    _proj_kernel(*ins, qr, kr, vr, gr, qs, kstage.at[pl.ds(slot, 1)], vstage.at[pl.ds(slot, 1)],
                 gs, ktbd, vbd)
    for c in copies(step):
        c.start()

    @pl.when(j == 0)
    def _():
        meta = [pltpu.make_async_copy(km_ref, kfull.at[i, pl.ds(0, n_meta), :], msem.at[0]),
                pltpu.make_async_copy(vm_ref, vfull.at[i, pl.ds(0, n_meta), :], msem.at[1])]
        for c in meta:
            c.start()
        for c in meta:
            c.wait()

    @pl.when(step == last)
    def _():
        for c in copies(step - 1) + copies(step):
            c.wait()


def _project(x, norm_w, w_in_bf, cos_t, sin_t, qnw2, knw2, g2, tm, meta_kv=None):
    b, t, _ = x.shape
    grid = (b, t // tm)
    row = lambda i, j: (i, j, 0)
    const2 = lambda i, j: (0, 0)
    out_block = pl.BlockSpec((1, tm, WIDTH), row)
    f32_out = jax.ShapeDtypeStruct((b, t, WIDTH), F32)
    bf_out = jax.ShapeDtypeStruct((b, t, WIDTH), BF16)
    in_specs = [
        pl.BlockSpec((1, tm, D_MODEL), row),
        pl.BlockSpec((1, D_MODEL), const2),
        pl.BlockSpec((D_MODEL, N_SEG * WIDTH), const2),
        pl.BlockSpec((tm, LANES), lambda i, j: (j, 0)),
        pl.BlockSpec((tm, LANES), lambda i, j: (j, 0)),
        pl.BlockSpec((1, LANES), const2),
        pl.BlockSpec((1, LANES), const2),
        pl.BlockSpec((2 * LANES, LANES), const2),
    ]
    out_specs = [out_block] * 8
    out_shape = [bf_out, bf_out, bf_out, f32_out, bf_out, f32_out, f32_out, f32_out]
    args = (x, norm_w, w_in_bf, cos_t, sin_t, qnw2, knw2, g2)
    kernel_fn, scratch = _proj_kernel, []
    if meta_kv is not None:
        assert b * (t // tm) >= 2
        n_meta = meta_kv[0].shape[0]
        nb = tm // CHUNK
        blk5 = lambda i, j: (i, 0, j, 0, 0)
        full = jax.ShapeDtypeStruct((b, n_meta + t, WIDTH), F32)
        in_specs += [pl.BlockSpec((n_meta, WIDTH), const2)] * 2
        out_specs[5] = out_specs[6] = pl.BlockSpec(memory_space=pl.ANY)
        out_shape[5] = out_shape[6] = full
        out_specs += [pl.BlockSpec((1, N_PAIR, nb, LANES, 2 * CHUNK), blk5),
                      pl.BlockSpec((1, N_PAIR, nb, 2 * CHUNK, LANES), blk5)]
        out_shape += [jax.ShapeDtypeStruct((b, N_PAIR, t // CHUNK, LANES, 2 * CHUNK), BF16),
                      jax.ShapeDtypeStruct((b, N_PAIR, t // CHUNK, 2 * CHUNK, LANES), BF16)]
        args += tuple(meta_kv)
        kernel_fn = _proj_prompt_kernel
        scratch = [pltpu.VMEM((2, tm, WIDTH), F32), pltpu.VMEM((2, tm, WIDTH), F32),
                   pltpu.SemaphoreType.DMA((2, 2)), pltpu.SemaphoreType.DMA((2,))]
    return pl.pallas_call(
        kernel_fn,
        grid=grid,
        in_specs=in_specs,
        out_specs=out_specs,
        out_shape=out_shape,
        scratch_shapes=scratch,
        compiler_params=pltpu.CompilerParams(
            dimension_semantics=("arbitrary", "arbitrary"),
            vmem_limit_bytes=VMEM_LIMIT),
        name="proj",
    )(*args)


RET_CHUNKS = 2


def _gate_norm_pair(o, g, w, g2):
    r = lax.rsqrt(_pair_head_mean(o * o, g2) + EPS)
    return ((o * r) * w) * _silu(g)


def _ret_kernel(q_ref, k_ref, v_ref, g_ref, km_ref, vm_ref, dmat_ref, cdec_ref, kdec_ref,
                kdecm_ref, gam_ref, bd_ref, w_ref, g2_ref, y_ref, sout_ref, state_ref):
    c = pl.program_id(1)
    bd = bd_ref[...] > 0.0
    g2 = g2_ref[...]
    lane = lax.broadcasted_iota(jnp.int32, (CHUNK, LANES), 1)
    low = lane < HEAD_DIM

    @pl.when(c == 0)
    def _():
        for p in range(N_PAIR):
            sl = slice(p * LANES, (p + 1) * LANES)
            kd = (km_ref[:, sl].astype(F32) * kdecm_ref[:, sl]).astype(BF16)
            state_ref[p] = jnp.where(bd, _tn_dot(kd, vm_ref[:, sl]), 0.0)

    zero = jnp.zeros((CHUNK, LANES), BF16)
    for u in range(RET_CHUNKS):
        rows = slice(u * CHUNK, (u + 1) * CHUNK)
        vs, sc, cross, upd, old = [], [], [], [], []
        for p in range(N_PAIR):
            sl = slice(p * LANES, (p + 1) * LANES)
            q = q_ref[0, rows, sl]
            k = k_ref[0, rows, sl]
            v = v_ref[0, rows, sl]
            s = state_ref[p]
            sc.append([_nt_dot(jnp.where(low, q, zero), k), _nt_dot(jnp.where(low, zero, q), k)])
            cross.append(jnp.dot(q, s.astype(BF16), preferred_element_type=F32))
            kd = (k.astype(F32) * kdec_ref[:, sl]).astype(BF16)
            upd.append(_tn_dot(kd, v))
            vs.append(v)
            old.append(s)
        for p in range(N_PAIR):
            state_ref[p] = gam_ref[p] * old[p] + jnp.where(bd, upd[p], 0.0)
            sc[p] = [(sc[p][e] * dmat_ref[2 * p + e]).astype(BF16) for e in range(2)]
        inner = [[jnp.dot(sc[p][e], vs[p], preferred_element_type=F32) for e in range(2)]
                 for p in range(N_PAIR)]
        for p in range(N_PAIR):
            sl = slice(p * LANES, (p + 1) * LANES)
            o = jnp.where(low, inner[p][0], inner[p][1]) + cross[p] * cdec_ref[:, sl]
            y_ref[0, rows, sl] = _gate_norm_pair(o, g_ref[0, rows, sl], w_ref[:, sl],
                                                 g2).astype(BF16)

    @pl.when(c == pl.num_programs(1) - 1)
    def _():
        sout_ref[0] = state_ref[...]


def _retention_prompt(qr, kr, vr, gr, kr_meta, vr_meta, tabs, w_onorm, g2):
    b, t, _ = qr.shape
    row = lambda i, j: (i, j, 0)
    c2 = lambda i, j: (0, 0)
    c3 = lambda i, j: (0, 0, 0)
    blk = pl.BlockSpec((1, RET_CHUNKS * CHUNK, WIDTH), row)
    return pl.pallas_call(
        _ret_kernel,
        grid=(b, t // (RET_CHUNKS * CHUNK)),
        in_specs=[
            blk, blk, blk, blk,
            pl.BlockSpec((N_META, WIDTH), c2),
            pl.BlockSpec((N_META, WIDTH), c2),
            pl.BlockSpec((N_HEADS, CHUNK, CHUNK), c3),
            pl.BlockSpec((CHUNK, WIDTH), c2),
            pl.BlockSpec((CHUNK, WIDTH), c2),
            pl.BlockSpec((N_META, WIDTH), c2),
            pl.BlockSpec((N_PAIR, LANES, LANES), c3),
            pl.BlockSpec((LANES, LANES), c2),
            pl.BlockSpec((1, WIDTH), c2),
            pl.BlockSpec((2 * LANES, LANES), c2),
        ],
        out_specs=[blk, pl.BlockSpec((1, N_PAIR, LANES, LANES), lambda i, j: (i, 0, 0, 0))],
        out_shape=[jax.ShapeDtypeStruct((b, t, WIDTH), BF16),
                   jax.ShapeDtypeStruct((b, N_PAIR, LANES, LANES), F32)],
        scratch_shapes=[pltpu.VMEM((N_PAIR, LANES, LANES), F32)],
        compiler_params=pltpu.CompilerParams(
            dimension_semantics=("arbitrary", "arbitrary"),
            vmem_limit_bytes=VMEM_LIMIT),
        name="ret_prompt",
    )(qr, kr, vr, gr, kr_meta, vr_meta, tabs["dmat"], tabs["cdec"], tabs["kdec"],
      tabs["kdec_meta"], tabs["gam"], tabs["bd"], w_onorm, g2)


SB_Q = 2 * CHUNK
SB_KB = 2
SB_SPLIT = 4


def _sb_step(q_ref, blocks, bias_ref, tt, acc_ref, carry_ref, side=None):
    def mask(vis_of, width):
        key = lax.broadcasted_iota(jnp.int32, (SB_Q, width), 1) % LANES
        return vis_of(lax.broadcasted_iota(jnp.int32, (SB_Q, width), 0), key)

    assert len(blocks) <= SB_KB
    side = side or [lambda: None] * 3
    side[0]()
    ls_all, lk_all = [], []
    for kt_of, _, vis_of in blocks:
        for p in range(N_PAIR):
            z = jnp.dot(q_ref[0, :, p * LANES:(p + 1) * LANES], kt_of(p),
                        preferred_element_type=F32) + bias_ref[p]
            ls, lk = _log2_sigmoids(z)
            if vis_of is not None:
                lk = jnp.where(mask(vis_of, 2 * LANES), lk, 0.0)
            ls_all.append(ls)
            lk_all.append(lk)
    lhs_all, tot_all = [], []
    zero = jnp.zeros((SB_Q, LANES), BF16)
    for p in range(N_PAIR):
        for e in range(2):
            sl = slice(e * LANES, (e + 1) * LANES)
            parts = [lk_all[n * N_PAIR + p][:, sl] for n in range(len(blocks))]
            tot_all.append(jnp.sum(sum(parts[1:], parts[0]), axis=1, keepdims=True))
            cols = [x.astype(BF16) for x in parts] + [zero] * (SB_KB - len(blocks))
            lhs_all.append(jnp.concatenate(cols, axis=1))
    side[1]()
    cc_all = [jnp.dot(lhs, tt, preferred_element_type=F32) for lhs in lhs_all]
    side[2]()
    for p in range(N_PAIR):
        a_run = []
        for n, (_, _, vis_of) in enumerate(blocks):
            for e in range(2):
                h = 2 * p + e
                a = jnp.exp2(ls_all[n * N_PAIR + p][:, e * LANES:(e + 1) * LANES]
                             + cc_all[h][:, n * LANES:(n + 1) * LANES] + carry_ref[h])
                if vis_of is not None:
                    a = jnp.where(mask(vis_of, LANES), a, 0.0)
                a_run.append(a.astype(BF16))
        for e in range(2):
            carry_ref[2 * p + e] += tot_all[2 * p + e]
        acc_ref[p] += jnp.dot(jnp.concatenate(a_run, axis=1),
                              jnp.concatenate([v_of(p) for _, v_of, _ in blocks], axis=0),
                              preferred_element_type=F32)


def _sb_prompt_part(k, q_ref, kt_ref, v_ref, ktm_ref, vm_ref, bias_ref, g_ref, w_ref, tt, g2,
                    y_ref, acc_ref, carry_ref, make_side):
    i = pl.program_id(1)

    def cached(blk, vis_of):
        return (lambda p: kt_ref[0, p, blk], lambda p: v_ref[0, p, blk], vis_of)

    @pl.when(k == 0)
    def _():
        acc_ref[...] = jnp.zeros_like(acc_ref)
        carry_ref[...] = jnp.zeros_like(carry_ref)
        _sb_step(q_ref,
                 [cached(2 * i + d, lambda r, s, d=d: s < r - d * CHUNK) for d in (1, 0)],
                 bias_ref, tt, acc_ref, carry_ref, make_side())

    def iteration(jj, side=None):
        newest = 2 * i - 1 - SB_KB * jj
        _sb_step(q_ref, [cached(newest - n, None) for n in range(SB_KB)],
                 bias_ref, tt, acc_ref, carry_ref, side)

    lo = (k * i) // SB_SPLIT
    hi = ((k + 1) * i) // SB_SPLIT
    between = (k > 0) & (k < SB_SPLIT - 1)

    @pl.when(between & (hi > lo))
    def _():
        iteration(lo, make_side())

    @pl.when(between & (hi == lo))
    def _():
        for stage in make_side():
            stage()

    def body(jj, carry):
        iteration(jj)
        return carry

    lax.fori_loop(lo + jnp.where(between, 1, 0), hi, body, 0)

    @pl.when(k == SB_SPLIT - 1)
    def _():
        _sb_step(q_ref, [(lambda p: ktm_ref[p], lambda p: vm_ref[p], lambda r, s: s < N_META)],
                 bias_ref, tt, acc_ref, carry_ref, make_side())
        for p in range(N_PAIR):
            sl = slice(p * LANES, (p + 1) * LANES)
            y_ref[0, :, sl] = _gate_norm_pair(acc_ref[p], g_ref[0, :, sl], w_ref[:, sl],
                                              g2).astype(BF16)


def _out_kernel(x_ref, yr_ref, ys_ref, w_ref, o_ref):
    o_ref[0] = (x_ref[0]
                + jnp.dot(yr_ref[0], w_ref[:WIDTH, :], preferred_element_type=F32)
                + jnp.dot(ys_ref[0], w_ref[WIDTH:, :], preferred_element_type=F32))


def _out_proj(x, yr, ys, w_out_bf, tm):
    b, t, _ = x.shape
    row = lambda i, j: (i, j, 0)
    return pl.pallas_call(
        _out_kernel,
        grid=(b, t // tm),
        in_specs=[
            pl.BlockSpec((1, tm, D_MODEL), row),
            pl.BlockSpec((1, tm, WIDTH), row),
            pl.BlockSpec((1, tm, WIDTH), row),
            pl.BlockSpec((2 * WIDTH, D_MODEL), lambda i, j: (0, 0)),
        ],
        out_specs=pl.BlockSpec((1, tm, D_MODEL), row),
        out_shape=jax.ShapeDtypeStruct(x.shape, F32),
        compiler_params=pltpu.CompilerParams(
            dimension_semantics=("arbitrary", "arbitrary"),
            vmem_limit_bytes=VMEM_LIMIT),
        name="out_proj",
    )(x, yr, ys, w_out_bf)


def _ret_sample_kernel(q_ref, k_ref, v_ref, g_ref, s_ref, dmat_ref, cdec_ref, kdec_ref,
                       gam_ref, w_ref, y_ref, sout_ref):
    n = q_ref.shape[0] * N_HEADS
    t = q_ref.shape[2]
    q = q_ref[...].reshape(n, t, HEAD_DIM)
    k = k_ref[...].reshape(n, t, HEAD_DIM)
    v = v_ref[...].reshape(n, t, HEAD_DIM)
    s = s_ref[...].reshape(n, HEAD_DIM, HEAD_DIM)
    sc = jnp.einsum("ntd,nsd->nts", q, k, preferred_element_type=F32) * dmat_ref[...]
    inner = jnp.einsum("nts,nse->nte", sc, v, preferred_element_type=F32)
    cross = jnp.einsum("ntd,nde->nte", q, s, preferred_element_type=F32) * cdec_ref[...]
    o = inner + cross
    upd = jnp.einsum("ntd,nte->nde", k * kdec_ref[...], v, preferred_element_type=F32)
    sout_ref[...] = (gam_ref[...] * s + upd).reshape(sout_ref.shape)
    r = lax.rsqrt(jnp.mean(o * o, axis=-1, keepdims=True) + EPS)
    g = g_ref[...].reshape(n, t, HEAD_DIM)
    y_ref[...] = (((o * r) * w_ref[...]) * _silu(g)).reshape(y_ref.shape)


def _retention_sample(q4, k4, v4, g4, state, tabs, w_heads, sblk):
    db, _, t, _ = q4.shape
    blk = pl.BlockSpec((sblk, N_HEADS, t, HEAD_DIM), lambda i: (i, 0, 0, 0))
    sblock = pl.BlockSpec((sblk, N_HEADS, HEAD_DIM, HEAD_DIM), lambda i: (i, 0, 0, 0))
    c3 = lambda i: (0, 0, 0)
    n = sblk * N_HEADS
    return pl.pallas_call(
        _ret_sample_kernel,
        grid=(db // sblk,),
        in_specs=[blk, blk, blk, blk, sblock,
                  pl.BlockSpec((n, t, t), c3),
                  pl.BlockSpec((n, t, 1), c3),
                  pl.BlockSpec((n, t, 1), c3),
                  pl.BlockSpec((n, 1, 1), c3),
                  pl.BlockSpec((n, 1, HEAD_DIM), c3)],
        out_specs=[blk, sblock],
        out_shape=[jax.ShapeDtypeStruct(q4.shape, F32),
                   jax.ShapeDtypeStruct(state.shape, F32)],
        compiler_params=pltpu.CompilerParams(
            dimension_semantics=("arbitrary",),
            vmem_limit_bytes=VMEM_LIMIT),
        name="ret_sample",
    )(q4, k4, v4, g4, state, tabs["dmat"], tabs["cdec"], tabs["kdec"], tabs["gam"], w_heads)


def _sample_stages(qblk, bias, k_refs, v_refs, tt, acc_ref, carry_ref, vis=None, first=False):
    n = len(k_refs)
    rows = qblk.shape[0]
    st = {}

    def scores():
        kts = [r[...].astype(BF16) for r in k_refs]
        st["z"] = jnp.dot(qblk, jnp.concatenate(kts, axis=1), preferred_element_type=F32)

    def keep_sums():
        ls, lk = _log2_sigmoids(st["z"] + jnp.concatenate([bias] * n, axis=1))
        if vis is not None:
            lk = jnp.where(vis, lk, 0.0)
        lk_rows = jnp.concatenate([lk[:, g * LANES:(g + 1) * LANES] for g in range(n)], axis=0)
        st["ls"] = ls
        st["cc"] = jnp.dot(_split_bf16(lk_rows), tt, preferred_element_type=F32)

    def weights():
        carry = jnp.zeros((rows, LANES), F32) if first else carry_ref[...]
        a_all = []
        for g in range(n):
            ccg = st["cc"][g * rows:(g + 1) * rows]
            a = jnp.exp2(st["ls"][:, g * LANES:(g + 1) * LANES] + ccg[:, :LANES] + carry)
            if vis is not None:
                a = jnp.where(vis, a, 0.0)
            a_all.append(a.astype(BF16))
            carry = carry + ccg[:, LANES:]
        a_cat = jnp.concatenate(a_all, axis=1)
        a_pad = jnp.concatenate([a_cat, jnp.zeros((LANES - rows, a_cat.shape[1]), BF16)], axis=0)
        vts = [r[...].astype(BF16) for r in v_refs]
        av = _nt_dot(jnp.concatenate(vts, axis=1), a_pad)
        acc_ref[...] = av if first else acc_ref[...] + av
        carry_ref[...] = carry

    return [scores, keep_sums, weights]


def _sample_new_tokens(qblk_ref, bias_ref, ktn_ref, vtn_ref, tt, t, acc_ref, carry_ref):
    rows = qblk_ref.shape[1]
    rr = lax.broadcasted_iota(jnp.int32, (rows, LANES), 0)
    col = lax.broadcasted_iota(jnp.int32, (rows, LANES), 1)
    for stage in _sample_stages(qblk_ref[0], bias_ref[...], [ktn_ref.at[0]], [vtn_ref.at[0]], tt,
                                acc_ref, carry_ref, vis=col < (rr % t), first=True):
        stage()


def _sample_finish(g_ref, w_ref, g2, y_ref, acc_ref):
    t = y_ref.shape[1]
    acc = acc_ref[...].T
    lane = lax.broadcasted_iota(jnp.int32, (t, WIDTH), 1)
    o = jnp.zeros((t, WIDTH), F32)
    for h in range(N_HEADS):
        o = o + jnp.where(lane // HEAD_DIM == h, acc[h * t:(h + 1) * t, :], 0.0)
    for p in range(N_PAIR):
        sl = slice(p * LANES, (p + 1) * LANES)
        y_ref[0, :, sl] = _gate_norm_pair(o[:, sl], g_ref[0, :, sl], w_ref[:, sl],
                                          g2).astype(BF16)


N_PROMPT_IN = 8
N_SAMPLE_IN = 6


def _page_copies(pt_ref, kt_hbm, vt_hbm, kbuf, vbuf, sem, step, n_group):
    n_pages = pt_ref.shape[1]
    slot = step % 2
    seq = step // SB_SPLIT
    newest = n_pages - 1 - (step % SB_SPLIT) * n_group
    copies = []
    for g in range(n_group):
        pid = pt_ref[seq, newest - g]
        copies.append(pltpu.make_async_copy(kt_hbm.at[pid], kbuf.at[slot, g], sem.at[slot, 0]))
        copies.append(pltpu.make_async_copy(vt_hbm.at[pid], vbuf.at[slot, g], sem.at[slot, 1]))
    return copies


def _sb_fused_kernel(pt_ref, w_ref, g2_ref, *rest):
    p_in = rest[:N_PROMPT_IN]
    s_in = rest[N_PROMPT_IN:N_PROMPT_IN + N_SAMPLE_IN]
    kt_hbm, vt_hbm = rest[N_PROMPT_IN + N_SAMPLE_IN:N_PROMPT_IN + N_SAMPLE_IN + 2]
    yp_ref, ys_ref, acc_ref, carry_ref, sacc_ref, scarry_ref, kbuf, vbuf, sem = rest[-9:]
    q_ref, kt_ref, v_ref, ktm_ref, vm_ref, bias_ref, g_ref, ttp_ref = p_in
    qblk_ref, sbias_ref, ktn_ref, vtn_ref, sg_ref, tts_ref = s_in
    n_group = kbuf.shape[1]
    k = pl.program_id(2)
    step = (pl.program_id(0) * pl.num_programs(1) + pl.program_id(1)) * SB_SPLIT + k
    n_steps = pl.num_programs(0) * pl.num_programs(1) * SB_SPLIT

    def copies(s):
        return _page_copies(pt_ref, kt_hbm, vt_hbm, kbuf, vbuf, sem, s, n_group)

    @pl.when(step == 0)
    def _():
        for c in copies(step):
            c.start()

    @pl.when(step + 1 < n_steps)
    def _():
        for c in copies(step + 1):
            c.start()

    for c in copies(step):
        c.wait()

    slot = step % 2
    g2 = g2_ref[...]
    tts = tts_ref[...]
    last = SB_SPLIT - 1

    def page_stages():
        return _sample_stages(qblk_ref[0], sbias_ref[...],
                              [kbuf.at[slot, g] for g in range(n_group)],
                              [vbuf.at[slot, g] for g in range(n_group)], tts, sacc_ref, scarry_ref)

    @pl.when(k == 0)
    def _():
        _sample_new_tokens(qblk_ref, sbias_ref, ktn_ref, vtn_ref, tts, ys_ref.shape[1],
                           sacc_ref, scarry_ref)

    _sb_prompt_part(k, q_ref, kt_ref, v_ref, ktm_ref, vm_ref, bias_ref, g_ref, w_ref,
                    ttp_ref[...], g2, yp_ref, acc_ref, carry_ref, page_stages)

    @pl.when(k == last)
    def _():
        _sample_finish(sg_ref, w_ref, g2, ys_ref, sacc_ref)


def _sb_fused(page_table, w_onorm, g2, qs, kt_bd, v_bd, ktm_bd, vm_bd, bias_pairs, gs, ttp,
              qblk, bias_rows, kt_new, vt_new, gs_s, tts, cache_kt, cache_vt):
    b, t, _ = qs.shape
    nblk = t // CHUNK
    n_q = t // SB_Q
    db, n_pages = page_table.shape
    ts = gs_s.shape[1]
    rows = qblk.shape[1]
    page = cache_kt.shape[2]
    n_group = n_pages // SB_SPLIT
    assert db == b * n_q and n_pages % SB_SPLIT == 0

    c2 = lambda i, j, k, pt: (0, 0)
    c3 = lambda i, j, k, pt: (0, 0, 0)
    qrow = lambda i, j, k, pt: (i, j, 0)
    batch5 = lambda i, j, k, pt: (i, 0, 0, 0, 0)
    seq3 = lambda i, j, k, pt: (i * n_q + j, 0, 0)
    qblock = pl.BlockSpec((1, SB_Q, WIDTH), qrow)
    once = pl.Buffered(1)
    grid_spec = pltpu.PrefetchScalarGridSpec(
        num_scalar_prefetch=1,
        grid=(b, n_q, SB_SPLIT),
        in_specs=[
            pl.BlockSpec((1, WIDTH), c2),
            pl.BlockSpec((2 * LANES, LANES), c2),
            qblock,
            pl.BlockSpec((1, N_PAIR, nblk, LANES, 2 * CHUNK), batch5, pipeline_mode=once),
            pl.BlockSpec((1, N_PAIR, nblk, 2 * CHUNK, LANES), batch5, pipeline_mode=once),
            pl.BlockSpec((N_PAIR, LANES, 2 * CHUNK), c3),
            pl.BlockSpec((N_PAIR, 2 * CHUNK, LANES), c3),
            pl.BlockSpec((N_PAIR, 1, 2 * LANES), c3),
            qblock,
            pl.BlockSpec((2 * LANES, 2 * LANES), c2),
            pl.BlockSpec((1, rows, WIDTH), seq3),
            pl.BlockSpec((rows, LANES), c2),
            pl.BlockSpec((1, WIDTH, LANES), seq3),
            pl.BlockSpec((1, WIDTH, LANES), seq3),
            pl.BlockSpec((1, ts, WIDTH), seq3),
            pl.BlockSpec((2 * LANES, 2 * LANES), c2),
            pl.BlockSpec(memory_space=pl.ANY),
            pl.BlockSpec(memory_space=pl.ANY),
        ],
        out_specs=[qblock, pl.BlockSpec((1, ts, WIDTH), seq3)],
        scratch_shapes=[pltpu.VMEM((N_PAIR, SB_Q, LANES), F32),
                        pltpu.VMEM((N_HEADS, SB_Q, LANES), F32),
                        pltpu.VMEM((WIDTH, LANES), F32),
                        pltpu.VMEM((rows, LANES), F32),
                        pltpu.VMEM((2, n_group, WIDTH, page), F32),
                        pltpu.VMEM((2, n_group, WIDTH, page), F32),
                        pltpu.SemaphoreType.DMA((2, 2))],
    )
    return pl.pallas_call(
        _sb_fused_kernel,
        grid_spec=grid_spec,
        out_shape=[jax.ShapeDtypeStruct((b, t, WIDTH), BF16),
                   jax.ShapeDtypeStruct((db, ts, WIDTH), BF16)],
        compiler_params=pltpu.CompilerParams(
            dimension_semantics=("arbitrary", "arbitrary", "arbitrary"),
            vmem_limit_bytes=VMEM_LIMIT),
        name="sb_fused",
    )(page_table, w_onorm, g2, qs, kt_bd, v_bd, ktm_bd, vm_bd, bias_pairs, gs, ttp,
      qblk, bias_rows, kt_new, vt_new, gs_s, tts, cache_kt, cache_vt)


def _rope_tables(pos):
    half = HEAD_DIM // 2
    inv = ROPE_BASE ** (-jnp.arange(half, dtype=F32) / half)
    ang = pos.astype(F32)[:, None] * inv[None, :]
    cos = jnp.tile(jnp.cos(ang), (1, LANES // half))
    sin = jnp.sin(ang)
    sin_signed = jnp.tile(jnp.concatenate([-sin, sin], axis=1), (1, LANES // HEAD_DIM))
    return cos, sin_signed


def _log_gamma():
    return jnp.log1p(-jnp.exp2(-5.0 - jnp.arange(N_HEADS, dtype=F32)))


def _head_lanes(x):
    return jnp.repeat(x, HEAD_DIM, axis=-1)


def _retention_tables_prompt():
    lg = _log_gamma()
    idx = jnp.arange(CHUNK, dtype=F32)
    rel = idx[:, None] - idx[None, :]
    dmat = jnp.where(rel[None] >= 0, jnp.exp(jnp.maximum(rel, 0.0)[None] * lg[:, None, None]), 0.0)
    cdec = _head_lanes(jnp.exp((idx + 1.0)[:, None] * lg[None, :]))
    kdec = _head_lanes(jnp.exp((CHUNK - 1.0 - idx)[:, None] * lg[None, :]))
    midx = jnp.arange(N_META, dtype=F32)
    kdec_meta = _head_lanes(jnp.exp((N_META - 1.0 - midx)[:, None] * lg[None, :]))
    head_of = jnp.arange(LANES) // HEAD_DIM
    bd = (head_of[:, None] == head_of[None, :]).astype(F32)
    gfull = jnp.exp(CHUNK * lg).reshape(N_PAIR, 2)
    gam = bd[None] * gfull[:, head_of][:, :, None]
    return {"dmat": dmat, "cdec": cdec, "kdec": kdec, "kdec_meta": kdec_meta, "gam": gam, "bd": bd}


def _retention_tables_sample(t, sblk):
    lg = _log_gamma()
    idx = jnp.arange(t, dtype=F32)
    rel = idx[:, None] - idx[None, :]
    dmat = jnp.where(rel[None] >= 0, jnp.exp(jnp.maximum(rel, 0.0)[None] * lg[:, None, None]), 0.0)
    cdec = jnp.exp((idx + 1.0)[None, :] * lg[:, None])[:, :, None]
    kdec = jnp.exp((t - 1.0 - idx)[None, :] * lg[:, None])[:, :, None]
    gam = jnp.exp(t * lg)[:, None, None]
    rep = lambda a: jnp.tile(a, (sblk, 1, 1))
    return {"dmat": rep(dmat), "cdec": rep(cdec), "kdec": rep(kdec), "gam": rep(gam)}


def _pair_mean_matrix():
    head_of = np.arange(LANES) // HEAD_DIM
    g = (head_of[:, None] == head_of[None, :]).astype(np.float32) / HEAD_DIM
    return jnp.asarray(np.concatenate([g, g], axis=0), dtype=BF16)


def _run_suffix_sum_matrix():
    i = np.arange(LANES)
    tri = (i[:, None] > i[None, :]).astype(np.float32)
    one = np.ones((LANES, LANES), np.float32)
    return jnp.asarray(np.block([[tri, one], [0 * one, tri]]), dtype=BF16)


def _suffix_sum_matrix_lanes():
    i = np.arange(LANES)
    tri = (i[:, None] > i[None, :]).astype(np.float32)
    half = np.concatenate([tri, np.ones((LANES, LANES), np.float32)], axis=1)
    return jnp.asarray(np.concatenate([half, half], axis=0), dtype=BF16)


def kernel(x_prompt, x_sample, cache_k, cache_v, state_ret, page_table, meta_tokens, norm_w, w_in,
           q_norm_w, k_norm_w, sb_bias, ret_onorm_w, sb_onorm_w, w_out):
    b, seq, _ = x_prompt.shape
    db, t, _ = x_sample.shape
    n_pool, page = cache_k.shape[1], cache_k.shape[2]
    n_pages = page_table.shape[1]
    past_len = n_pages * page

    w_in_bf = w_in[0].astype(BF16)
    w_out_bf = w_out[0].astype(BF16)
    nw = norm_w[0][None, :]
    qnw2 = jnp.tile(q_norm_w[0], LANES // HEAD_DIM)[None, :]
    knw2 = jnp.tile(k_norm_w[0], LANES // HEAD_DIM)[None, :]
    w_ret = ret_onorm_w[0][None, :]
    w_sb = sb_onorm_w[0][None, :]
    bias = sb_bias[0] * LOG2E
    g2 = _pair_mean_matrix()
    tt = _suffix_sum_matrix_lanes()

    cos_m, sin_m = _rope_tables(jnp.arange(N_META))
    cos_p, sin_p = _rope_tables(N_META + jnp.arange(seq))
    _, kr_m, vr_m, _, _, ks_m, vs_m, _ = _project(
        meta_tokens[None], nw, w_in_bf, cos_m, sin_m, qnw2, knw2, g2, N_META)
    qr, kr, vr, gr, qs, k_full, v_full, gs, kt_bd, v_bd = _project(
        x_prompt, nw, w_in_bf, cos_p, sin_p, qnw2, knw2, g2, 256, meta_kv=(ks_m[0], vs_m[0]))

    tabs = _retention_tables_prompt()
    y_r, s_pairs = _retention_prompt(qr, kr, vr, gr, kr_m[0], vr_m[0], tabs, w_ret, g2)

    pad = ((0, CHUNK - N_META), (0, 0))
    km = jnp.pad(ks_m[0], pad).astype(BF16).reshape(CHUNK, N_PAIR, 2, HEAD_DIM)
    vm = jnp.pad(vs_m[0], pad).astype(BF16).reshape(CHUNK, N_PAIR, 2, HEAD_DIM)
    eye2 = jnp.eye(2, dtype=BF16)
    ktm_bd = (km.transpose(1, 2, 3, 0)[:, :, :, None, :] * eye2[None, :, None, :, None]
              ).reshape(N_PAIR, LANES, 2 * CHUNK)
    vm_bd = (vm.transpose(1, 0, 2, 3)[:, None, :, :, :] * eye2[None, :, None, :, None]
             ).reshape(N_PAIR, 2 * CHUNK, LANES)
    bias_pairs = jnp.repeat(bias.reshape(N_PAIR, 1, 2), LANES, axis=2)

    new_k_prompt = k_full.reshape(1, b, seq + N_META, N_HEADS, HEAD_DIM)
    new_v_prompt = v_full.reshape(1, b, seq + N_META, N_HEADS, HEAD_DIM)
    sp = s_pairs.reshape(b, N_PAIR, 2, HEAD_DIM, 2, HEAD_DIM)
    new_state_prompt = jnp.stack([sp[:, :, 0, :, 0, :], sp[:, :, 1, :, 1, :]], axis=2)
    new_state_prompt = new_state_prompt.reshape(1, b, N_HEADS, HEAD_DIM, HEAD_DIM)

    n_tok = db * t
    cos_s, sin_s = _rope_tables(jnp.tile(past_len + jnp.arange(t), db))
    qr_s, kr_s, vr_s, gr_s, qs_s, ks_s, vs_s, gs_s = _project(
        x_sample.reshape(1, n_tok, D_MODEL), nw, w_in_bf, cos_s, sin_s, qnw2, knw2, g2, 256)

    def heads_major(a):
        return a.reshape(db, t, N_HEADS, HEAD_DIM).transpose(0, 2, 1, 3).astype(F32)

    sblk = 8
    tabs_s = _retention_tables_sample(t, sblk)
    w_heads = jnp.tile(ret_onorm_w[0].reshape(N_HEADS, 1, HEAD_DIM), (sblk, 1, 1))
    y_r4, new_state_sample = _retention_sample(
        heads_major(qr_s), heads_major(kr_s), heads_major(vr_s), heads_major(gr_s),
        state_ret[0], tabs_s, w_heads, sblk)
    y_r_s = y_r4.transpose(0, 2, 1, 3).reshape(1, n_tok, WIDTH).astype(BF16)

    q4 = qs_s.reshape(db, t, N_HEADS, HEAD_DIM).transpose(0, 2, 1, 3)
    eye = jnp.eye(N_HEADS, dtype=BF16)
    qblk = (q4[:, :, :, None, :] * eye[None, :, None, :, None]).reshape(db, N_HEADS * t, WIDTH)
    bias_rows = jnp.broadcast_to(jnp.repeat(bias, t)[:, None], (N_HEADS * t, LANES))

    def new_tokens_t(a):
        at = a.reshape(db, t, WIDTH).transpose(0, 2, 1).astype(BF16)
        return jnp.pad(at, ((0, 0), (0, 0), (0, LANES - t)))

    def pages_t(c):
        return c[0].transpose(0, 2, 3, 1).reshape(n_pool, WIDTH, page)

    y_s, y_s_s = _sb_fused(page_table, w_sb, g2,
                           qs, kt_bd, v_bd, ktm_bd, vm_bd, bias_pairs, gs,
                           _run_suffix_sum_matrix(),
                           qblk, bias_rows, new_tokens_t(ks_s[0]), new_tokens_t(vs_s[0]),
                           gs_s.reshape(db, t, WIDTH), tt, pages_t(cache_k), pages_t(cache_v))
    y_prompt = _out_proj(x_prompt, y_r, y_s, w_out_bf, 512)
    y_sample = _out_proj(x_sample.reshape(1, n_tok, D_MODEL), y_r_s,
                         y_s_s.reshape(1, n_tok, WIDTH), w_out_bf, 512)
    y_sample = y_sample.reshape(db, t, D_MODEL)

    new_k_sample = ks_s.reshape(1, db, t, N_HEADS, HEAD_DIM)
    new_v_sample = vs_s.reshape(1, db, t, N_HEADS, HEAD_DIM)
    return (y_prompt, y_sample, new_k_prompt, new_v_prompt, new_state_prompt,
            new_k_sample, new_v_sample, new_state_sample[None])
```
